```python
import jax, jax.numpy as jnp
from jax import lax
import numpy as np

D_MODEL = 2048
BATCH = 2
SEQ = 8192
DEPTH = 1

MEM_LEN = 256
EPS = 1e-6
RWKV_DIM = 1024
RWKV_HEAD = 64
RWKV_HEADS = RWKV_DIM // RWKV_HEAD
LORA_W = 64
LORA_A = 64
LORA_G = 160
RWKV_COLS = 3 * RWKV_DIM + LORA_W + LORA_A + LORA_G
GN_EPS = 64e-5
LRU_DIM = 1024
LRU_BLOCKS = 16
LRU_BW = LRU_DIM // LRU_BLOCKS
CONV_W = 4
RG_C = 8.0
XA_HEADS = 4
XA_HEAD_DIM = 256
XA_DIM = XA_HEADS * XA_HEAD_DIM
N_BRANCH = 3
IN_COLS = RWKV_COLS + 2 * LRU_DIM + XA_DIM + N_BRANCH * D_MODEL
N_EXPERTS = 32
TOP_K = 4
D_EXPERT = D_MODEL
SWIGLU_LIMIT = 7.0
SWIGLU_ALPHA = 1.702
EXPERT_BLOCK = 512

kernel_name = "hybrid_rwkv7_rglru_memxattn_moe_block"


def rms_norm(x, g):
    xf = x.astype(jnp.float32)
    return xf * lax.rsqrt(jnp.mean(xf * xf, axis=-1, keepdims=True) + EPS) * g.astype(jnp.float32)


def token_shift(p):
    return jnp.pad(p, ((0, 0), (1, 0), (0, 0)))[:, :-1]


def rwkv7_scan(r, w, k, v, a, b):
    bsz = r.shape[0]

    def step(state, inp):
        r_t, w_t, k_t, v_t, a_t, b_t = inp
        sa = jnp.einsum('bhvk,bhk->bhv', state, a_t)
        state = (state * w_t[:, :, None, :] + sa[..., None] * b_t[:, :, None, :]
                 + v_t[..., None] * k_t[:, :, None, :])
        return state, jnp.einsum('bhvk,bhk->bhv', state, r_t)

    xs = tuple(jnp.moveaxis(t, 1, 0) for t in (r, w, k, v, a, b))
    s0 = jnp.zeros((bsz, RWKV_HEADS, RWKV_HEAD, RWKV_HEAD), jnp.float32)
    _, y = lax.scan(step, s0, xs)
    return jnp.moveaxis(y, 0, 1)


def rwkv7_branch(p, mu, w0, w2, a0, a2, g2, k_k, k_a, r_k, ln_w, ln_b):
    bsz, seq, _ = p.shape
    p = p + (token_shift(p) - p) * mu
    c = RWKV_DIM
    r, k, v, w_lo, a_lo, g_lo = jnp.split(
        p, [c, 2 * c, 3 * c, 3 * c + LORA_W, 3 * c + LORA_W + LORA_A], axis=-1)
    w = -jax.nn.softplus(-(w0 + jnp.tanh(w_lo) @ w2)) - 0.5
    decay = jnp.exp(-jnp.exp(w))
    a = jax.nn.sigmoid(a0 + a_lo @ a2)
    g = jax.nn.sigmoid(g_lo) @ g2
    hs = (bsz, seq, RWKV_HEADS, RWKV_HEAD)
    kk = (k * k_k).reshape(hs)
    kk = kk * lax.rsqrt(jnp.maximum(jnp.sum(kk * kk, axis=-1, keepdims=True), 1e-24))
    a_h = a.reshape(hs)
    k = (k * (1.0 + (a - 1.0) * k_a)).reshape(hs)
    r = r.reshape(hs)
    v = v.reshape(hs)
    y = rwkv7_scan(r, decay.reshape(hs), k, v, -kk, kk * a_h)
    mean = jnp.mean(y, axis=-1, keepdims=True)
    var = jnp.mean(jnp.square(y - mean), axis=-1, keepdims=True)
    y = ((y - mean) * lax.rsqrt(var + GN_EPS)).reshape(bsz, seq, RWKV_DIM) * ln_w + ln_b
    bonus = jnp.sum(r * k * r_k, axis=-1, keepdims=True) * v
    return (y + bonus.reshape(bsz, seq, RWKV_DIM)) * g


def causal_conv(x, w, b):
    seq = x.shape[1]
    xp = jnp.pad(x, ((0, 0), (CONV_W - 1, 0), (0, 0)))
    y = b + xp[:, CONV_W - 1:CONV_W - 1 + seq] * w[0]
    for j in range(1, CONV_W):
        y = y + xp[:, CONV_W - 1 - j:CONV_W - 1 - j + seq] * w[j]
    return y


def rglru_branch(px, py, conv_w, conv_b, w_r, b_r, w_i, b_i, lam):
    bsz, seq, _ = px.shape
    xc = causal_conv(px, conv_w, conv_b)
    xb = xc.reshape(bsz, seq, LRU_BLOCKS, LRU_BW)
    gate_r = jax.nn.sigmoid(jnp.einsum('bsnw,nwv->bsnv', xb, w_r).reshape(bsz, seq, LRU_DIM) + b_r)
    gate_i = jax.nn.sigmoid(jnp.einsum('bsnw,nwv->bsnv', xb, w_i).reshape(bsz, seq, LRU_DIM) + b_i)
    log_a = -RG_C * gate_r * jax.nn.softplus(-lam)
    a = jnp.exp(log_a)
    mult = jnp.sqrt(jnp.maximum(1.0 - jnp.exp(2.0 * log_a), 0.0))
    mult = mult.at[:, 0].set(1.0)
    bx = mult * gate_i * xc

    def combine(lhs, rhs):
        a1, b1 = lhs
        a2, b2 = rhs
        return a1 * a2, a2 * b1 + b2

    _, h = lax.associative_scan(combine, (a, bx), axis=1)
    return h * jax.nn.gelu(py)


def memory_xattn(q_cols, mem, g_mem, w_mk, w_mv):
    bsz, seq, _ = q_cols.shape
    mn = rms_norm(mem, g_mem)
    q = q_cols.reshape(bsz, seq, XA_HEADS, XA_HEAD_DIM)
    k = (mn @ w_mk).reshape(bsz, -1, XA_HEADS, XA_HEAD_DIM)
    v = (mn @ w_mv).reshape(bsz, -1, XA_HEADS, XA_HEAD_DIM)
    s = jnp.einsum('bshd,bmhd->bhsm', q, k).astype(jnp.float32) * (XA_HEAD_DIM ** -0.5)
    pr = jax.nn.softmax(s, axis=-1)
    return jnp.einsum('bhsm,bmhd->bshd', pr, v).reshape(bsz, seq, XA_DIM)


def moe(xn, w_router, b_router, w_gate, b_gate, w_up, b_up, w_down, b_down):
    bsz, seq, d = xn.shape
    n_tok = bsz * seq
    n_assign = n_tok * TOP_K
    xt = xn.reshape(n_tok, d)
    logits = (xt @ w_router).astype(jnp.float32) + b_router
    top_val, top_idx = lax.top_k(logits, TOP_K)
    gates = jax.nn.softmax(top_val, axis=-1)
    e_flat = top_idx.reshape(-1)
    tok_flat = jnp.arange(n_assign, dtype=jnp.int32) // TOP_K
    g_flat = gates.reshape(-1)
    order = jnp.argsort(e_flat)
    e_sorted = e_flat[order]
    tok_sorted = tok_flat[order]
    g_sorted = g_flat[order]
    counts = jnp.bincount(e_flat, length=N_EXPERTS)
    padded = ((counts + EXPERT_BLOCK - 1) // EXPERT_BLOCK) * EXPERT_BLOCK
    start = jnp.cumsum(counts) - counts
    pend = jnp.cumsum(padded)
    pstart = pend - padded
    rank = jnp.arange(n_assign, dtype=jnp.int32) - start[e_sorted]
    dest = pstart[e_sorted] + rank
    n_rows = (-(-n_assign // EXPERT_BLOCK)) * EXPERT_BLOCK + N_EXPERTS * EXPERT_BLOCK
    n_blocks = n_rows // EXPERT_BLOCK
    rows = jnp.zeros((n_rows, d), xt.dtype).at[dest].set(xt[tok_sorted])
    block_start = jnp.arange(n_blocks, dtype=jnp.int32) * EXPERT_BLOCK
    block_e = jnp.minimum(jnp.searchsorted(pend, block_start, side='right'), N_EXPERTS - 1)

    def expert_block(args):
        xb, e = args
        hg = jnp.minimum(xb @ w_gate[e] + b_gate[e], SWIGLU_LIMIT)
        hu = jnp.clip(xb @ w_up[e] + b_up[e], -SWIGLU_LIMIT, SWIGLU_LIMIT)
        hh = (hu + 1.0) * hg * jax.nn.sigmoid(SWIGLU_ALPHA * hg)
        return hh @ w_down[e] + b_down[e]

    out_rows = lax.map(expert_block, (rows.reshape(n_blocks, EXPERT_BLOCK, d), block_e))
    out_rows = out_rows.reshape(n_rows, d)
    contrib = out_rows[dest].astype(jnp.float32) * g_sorted[:, None]
    y = jnp.zeros((n_tok, d), jnp.float32).at[tok_sorted].add(contrib)
    return y.reshape(bsz, seq, d)


def setup_inputs(seed: int = 0) -> dict:
    key = jax.random.key(seed)
    ks = iter(jax.random.split(key, 64))
    f32 = jnp.float32

    def nrm(shape, scale):
        return jax.random.normal(next(ks), shape, f32) * scale

    def gain(shape):
        return 1.0 + nrm(shape, 0.05)

    L = DEPTH
    x = nrm((BATCH, SEQ, D_MODEL), 1.0)
    mem = nrm((BATCH, MEM_LEN, D_MODEL), 1.0)
    lam_u = jax.random.uniform(next(ks), (L, LRU_DIM), f32, 0.9, 0.999)
    mu = jax.random.uniform(next(ks), (L, RWKV_COLS), f32)
    return {
        "x": x,
        "mem": mem,
        "g_mix": gain((L, D_MODEL)),
        "w_in": nrm((L, D_MODEL, IN_COLS), D_MODEL ** -0.5),
        "mu_rwkv": mu,
        "rwkv_w0": jnp.linspace(-6.0, -1.0, RWKV_DIM, dtype=f32)[None] + nrm((L, RWKV_DIM), 0.1),
        "rwkv_w2": nrm((L, LORA_W, RWKV_DIM), 0.1 * LORA_W ** -0.5),
        "rwkv_a0": nrm((L, RWKV_DIM), 0.1),
        "rwkv_a2": nrm((L, LORA_A, RWKV_DIM), LORA_A ** -0.5),
        "rwkv_g2": nrm((L, LORA_G, RWKV_DIM), LORA_G ** -0.5),
        "rwkv_k_k": 0.85 + nrm((L, RWKV_DIM), 0.05),
        "rwkv_k_a": 1.0 + nrm((L, RWKV_DIM), 0.05),
        "rwkv_r_k": nrm((L, RWKV_HEADS, RWKV_HEAD), 0.1),
        "rwkv_ln_w": gain((L, RWKV_DIM)),
        "rwkv_ln_b": nrm((L, RWKV_DIM), 0.02),
        "w_rwkv_out": nrm((L, RWKV_DIM, D_MODEL), RWKV_DIM ** -0.5),
        "lru_conv_w": nrm((L, CONV_W, LRU_DIM), CONV_W ** -0.5),
        "lru_conv_b": nrm((L, LRU_DIM), 0.02),
        "lru_w_r": nrm((L, LRU_BLOCKS, LRU_BW, LRU_BW), LRU_BW ** -0.5),
        "lru_b_r": nrm((L, LRU_DIM), 0.02),
        "lru_w_i": nrm((L, LRU_BLOCKS, LRU_BW, LRU_BW), LRU_BW ** -0.5),
        "lru_b_i": nrm((L, LRU_DIM), 0.02),
        "lru_lambda": jnp.log(lam_u) - jnp.log1p(-lam_u),
        "w_lru_out": nrm((L, LRU_DIM, D_MODEL), LRU_DIM ** -0.5),
        "g_mem": gain((L, D_MODEL)),
        "w_mem_k": nrm((L, D_MODEL, XA_DIM), D_MODEL ** -0.5),
        "w_mem_v": nrm((L, D_MODEL, XA_DIM), D_MODEL ** -0.5),
        "w_xa_out": nrm((L, XA_DIM, D_MODEL), XA_DIM ** -0.5),
        "w_o": nrm((L, D_MODEL, D_MODEL), D_MODEL ** -0.5),
        "g_ffn": gain((L, D_MODEL)),
        "w_router": nrm((L, D_MODEL, N_EXPERTS), D_MODEL ** -0.5),
        "b_router": nrm((L, N_EXPERTS), 0.01),
        "w_gate": nrm((L, N_EXPERTS, D_MODEL, D_EXPERT), D_MODEL ** -0.5),
        "b_gate": nrm((L, N_EXPERTS, D_EXPERT), 0.02),
        "w_up": nrm((L, N_EXPERTS, D_MODEL, D_EXPERT), D_MODEL ** -0.5),
        "b_up": nrm((L, N_EXPERTS, D_EXPERT), 0.02),
        "w_down": nrm((L, N_EXPERTS, D_EXPERT, D_MODEL), D_EXPERT ** -0.5),
        "b_down": nrm((L, N_EXPERTS, D_MODEL), 0.02),
        "g_final": gain((D_MODEL,)),
    }


def reference(x, mem, g_mix, w_in, mu_rwkv, rwkv_w0, rwkv_w2, rwkv_a0, rwkv_a2, rwkv_g2,
              rwkv_k_k, rwkv_k_a, rwkv_r_k, rwkv_ln_w, rwkv_ln_b, w_rwkv_out,
              lru_conv_w, lru_conv_b, lru_w_r, lru_b_r, lru_w_i, lru_b_i, lru_lambda, w_lru_out,
              g_mem, w_mem_k, w_mem_v, w_xa_out, w_o, g_ffn, w_router, b_router,
              w_gate, b_gate, w_up, b_up, w_down, b_down, g_final):
    bsz, seq, _ = x.shape
    h = x.astype(jnp.float32)
    c0 = RWKV_COLS
    c1 = c0 + LRU_DIM
    c2 = c1 + LRU_DIM
    c3 = c2 + XA_DIM
    for l in range(DEPTH):
        xn = rms_norm(h, g_mix[l])
        p = xn @ w_in[l]
        p_rwkv, p_lx, p_ly, p_q, p_gate = jnp.split(p, [c0, c1, c2, c3], axis=-1)
        y_a = rwkv7_branch(p_rwkv, mu_rwkv[l], rwkv_w0[l], rwkv_w2[l], rwkv_a0[l], rwkv_a2[l],
                           rwkv_g2[l], rwkv_k_k[l], rwkv_k_a[l], rwkv_r_k[l], rwkv_ln_w[l], rwkv_ln_b[l])
        y_b = rglru_branch(p_lx, p_ly, lru_conv_w[l], lru_conv_b[l], lru_w_r[l], lru_b_r[l],
                           lru_w_i[l], lru_b_i[l], lru_lambda[l])
        y_c = memory_xattn(p_q, mem, g_mem[l], w_mem_k[l], w_mem_v[l])
        gates = jax.nn.sigmoid(p_gate.astype(jnp.float32)).reshape(bsz, seq, N_BRANCH, D_MODEL)
        merged = (gates[:, :, 0] * (y_a @ w_rwkv_out[l])
                  + gates[:, :, 1] * (y_b @ w_lru_out[l])
                  + gates[:, :, 2] * (y_c @ w_xa_out[l]))
        h = h + (merged @ w_o[l]).astype(jnp.float32)
        h = h + moe(rms_norm(h, g_ffn[l]), w_router[l], b_router[l], w_gate[l], b_gate[l],
                    w_up[l], b_up[l], w_down[l], b_down[l])
    return rms_norm(h, g_final).astype(x.dtype)
```

```python
import functools

import jax
import jax.numpy as jnp
from jax import lax
from jax.experimental import pallas as pl
from jax.experimental.pallas import tpu as pltpu

F32 = jnp.float32
BF16 = jnp.bfloat16

EPS = 1e-6
GN_EPS = 64e-5
RG_C = 8.0
SWIGLU_LIMIT = 7.0
SWIGLU_ALPHA = 1.702
RWKV_HEAD = 64
LORA_W = 64
LORA_A = 64
LORA_G = 160
CONV_W = 4
LRU_BLOCKS = 16
XA_HEADS = 4
TOP_K = 4
N_BRANCH = 3

LANES = 128
SUBLANES = 8
CHUNK = 64
SUB = 16
ROW_BLOCK = 512
LORA_PAD = 512
IDX_PAD = 1024
NEG_BIG = -1e30
VMEM_LIMIT = 56 * 1024 * 1024


def _cp(sem, vmem=None):
    return pltpu.CompilerParams(dimension_semantics=sem, vmem_limit_bytes=vmem or VMEM_LIMIT)


def _mm(a, b):
    return jnp.dot(a, b, preferred_element_type=F32)


def _mm_nt(a, b):
    return lax.dot_general(a, b, (((1,), (1,)), ((), ())), preferred_element_type=F32)


def _mm_tn(a, b):
    return lax.dot_general(a, b, (((0,), (0,)), ((), ())), preferred_element_type=F32)


def _split(x):
    hi = x.astype(BF16)
    lo = (x - hi.astype(F32)).astype(BF16)
    return hi, lo


def _mm_exact_rhs(x, m):
    hi, lo = _split(x)
    return _mm(hi, m) + _mm(lo, m)


def _mm_exact_lhs(m, x):
    hi, lo = _split(x)
    return _mm(m, hi) + _mm(m, lo)


def _sigmoid(x):
    return 1.0 / (1.0 + jnp.exp(-x))


def _softplus(x):
    return jnp.maximum(x, 0.0) + jnp.log(1.0 + jnp.exp(-jnp.abs(x)))


def _norm_kernel(x_ref, g_ref, o_ref):
    x = x_ref[...].astype(F32)
    ms = jnp.mean(x * x, axis=-1, keepdims=True)
    o_ref[...] = (x * lax.rsqrt(ms + EPS) * g_ref[...]).astype(o_ref.dtype)


def norm_cast(x, g, tm, out_dtype=BF16):
    t, d = x.shape
    return pl.pallas_call(
        _norm_kernel,
        grid=(t // tm,),
        in_specs=[pl.BlockSpec((tm, d), lambda i: (i, 0)), pl.BlockSpec((1, d), lambda i: (0, 0))],
        out_specs=pl.BlockSpec((tm, d), lambda i: (i, 0)),
        out_shape=jax.ShapeDtypeStruct((t, d), out_dtype),
        compiler_params=_cp(("parallel",)),
        name="norm_cast",
    )(x, g.reshape(1, d).astype(F32))


def _matmul_kernel(a_ref, w_ref, o_ref):
    o_ref[...] = _mm(a_ref[...], w_ref[...]).astype(o_ref.dtype)


def matmul(a, w, tm, tn, out_dtype):
    m, k = a.shape
    n = w.shape[1]
    return pl.pallas_call(
        _matmul_kernel,
        grid=(m // tm, n // tn),
        in_specs=[pl.BlockSpec((tm, k), lambda i, j: (i, 0)), pl.BlockSpec((k, tn), lambda i, j: (0, j))],
        out_specs=pl.BlockSpec((tm, tn), lambda i, j: (i, j)),
        out_shape=jax.ShapeDtypeStruct((m, n), out_dtype),
        compiler_params=_cp(("parallel", "parallel")),
        name="matmul",
    )(a, w)


def _rwkv_kernel(r_ref, k_ref, v_ref, lo_ref, rp_ref, kp_ref, vp_ref, lop_ref,
                 mur_ref, muk_ref, muv_ref, mulo_ref, w0_ref, a0_ref, kk_ref, ka_ref, rk_ref,
                 lnw_ref, lnb_ref, w2_ref, a2_ref, g2_ref, o_ref, s_ref, *, tm):
    i = pl.program_id(2)
    first = i == 0

    @pl.when(first)
    def _():
        s_ref[...] = jnp.zeros_like(s_ref)

    def mix(x_ref, p_ref, mu_ref):
        x = x_ref[...]
        prev = jnp.where(first, 0.0, p_ref[SUBLANES - 1:SUBLANES, :])
        row = lax.broadcasted_iota(jnp.int32, x.shape, 0)
        sh = jnp.where(row == 0, prev, pltpu.roll(x, 1, axis=0))
        return x + (sh - x) * mu_ref[...]

    r = mix(r_ref, rp_ref, mur_ref)
    k = mix(k_ref, kp_ref, muk_ref)
    v = mix(v_ref, vp_ref, muv_ref)
    lo = mix(lo_ref, lop_ref, mulo_ref)

    lo_wa = lo[:, 0:LANES]
    wl = _mm(jnp.tanh(lo_wa).astype(BF16), w2_ref[...])
    al = _mm(lo_wa.astype(BF16), a2_ref[...])
    g = _mm(_sigmoid(lo[:, LANES:3 * LANES]).astype(BF16), g2_ref[...])

    w = -_softplus(-(w0_ref[...] + wl)) - 0.5
    ld = -jnp.exp(w)
    a_sig = _sigmoid(a0_ref[...] + al)

    lane = lax.broadcasted_iota(jnp.int32, (LANES, LANES), 1)
    sub = lax.broadcasted_iota(jnp.int32, (LANES, LANES), 0)
    e2 = jnp.where((lane // RWKV_HEAD) == (sub // RWKV_HEAD), 1.0, 0.0).astype(BF16)

    kk = k * kk_ref[...]
    kk = kk * lax.rsqrt(jnp.maximum(_mm_exact_rhs(kk * kk, e2), 1e-24))
    k2 = k * (1.0 + (a_sig - 1.0) * ka_ref[...])
    a_vec = -kk
    b_vec = kk * a_sig

    trow = lax.broadcasted_iota(jnp.int32, (tm, tm), 0)
    tcol = lax.broadcasted_iota(jnp.int32, (tm, tm), 1)
    tri = jnp.where(((trow // CHUNK) == (tcol // CHUNK)) & (tcol <= trow), 1.0, 0.0).astype(BF16)
    c = _mm_exact_lhs(tri, ld)

    lane_row = lax.broadcasted_iota(jnp.int32, (CHUNK, LANES), 1)
    head_a = lane_row < RWKV_HEAD

    def stack(x):
        return jnp.concatenate([jnp.where(head_a, x, 0.0), jnp.where(head_a, 0.0, x)], axis=0)

    n2 = 2 * CHUNK
    ri = lax.broadcasted_iota(jnp.int32, (n2, n2), 0)
    ci = lax.broadcasted_iota(jnp.int32, (n2, n2), 1)
    strict = ri > ci
    incl = ri >= ci
    blk = (ri // SUB) == (ci // SUB)
    eye = jnp.where(ri == ci, 1.0, 0.0)

    s = s_ref[...]
    ys = []
    for cidx in range(tm // CHUNK):
        sl = slice(cidx * CHUNK, (cidx + 1) * CHUNK)
        c_ = c[sl]
        c_end = c_[CHUNK - 1:CHUNK, :]
        e_pos = jnp.exp(c_)
        e_neg = jnp.exp(-c_)
        e_end = jnp.exp(c_end - c_)
        p_end = jnp.exp(c_end)
        rt32 = stack(r[sl] * e_pos)
        at = stack(a_vec[sl] * jnp.exp(c_ - ld[sl])).astype(BF16)
        rt = rt32.astype(BF16)
        bt = stack(b_vec[sl] * e_neg).astype(BF16)
        kt = stack(k2[sl] * e_neg).astype(BF16)
        bh = stack(b_vec[sl] * e_end).astype(BF16)
        kh = stack(k2[sl] * e_end).astype(BF16)
        vs = stack(v[sl]).astype(BF16)

        gram = _mm_nt(jnp.concatenate([at, rt], axis=0), jnp.concatenate([bt, kt], axis=0))
        l_ab = jnp.where(strict, gram[0:n2, 0:n2], 0.0)
        l_ak = jnp.where(strict, gram[0:n2, n2:], 0.0)
        m_rb = jnp.where(incl, gram[n2:, 0:n2], 0.0)
        m_rk = jnp.where(incl, gram[n2:, n2:], 0.0)

        d = jnp.where(blk, l_ab, 0.0)
        l_off = l_ab - d
        db = d.astype(BF16)
        p = eye + d
        d2 = _mm(db, db).astype(BF16)
        p = p + _mm(d2, p.astype(BF16))
        d4 = _mm(d2, d2).astype(BF16)
        p = p + _mm(d4, p.astype(BF16))
        d8 = _mm(d4, d4).astype(BF16)
        p = p + _mm(d8, p.astype(BF16))
        td = p.astype(BF16)
        mm = _mm(td, l_off.astype(BF16)).astype(BF16)
        lv = _mm(l_ak.astype(BF16), vs)
        x = _mm(td, jnp.concatenate([at, lv.astype(BF16)], axis=1))
        x = x + _mm(mm, x.astype(BF16))
        m2 = _mm(mm, mm).astype(BF16)
        x = x + _mm(m2, x.astype(BF16))
        xb = x.astype(BF16)
        wb = xb[:, 0:n2]
        u0b = xb[:, n2:]

        t1 = _mm(m_rb.astype(BF16), xb)
        rp = (rt32 + t1[:, 0:n2]).astype(BF16)
        y0 = t1[:, n2:] + _mm(m_rk.astype(BF16), vs)
        ab = _mm_tn(bh, wb).astype(BF16)
        gt = _mm_tn(u0b, bh) + _mm_tn(vs, kh)

        sb = s.astype(BF16)
        yst = _mm_nt(rp, sb) + y0
        s = s * p_end + _mm_nt(sb, ab) + gt
        ys.append(yst[0:CHUNK] + yst[CHUNK:n2])
    s_ref[...] = s
    y = jnp.concatenate(ys, axis=0) if len(ys) > 1 else ys[0]

    inv_n = 1.0 / RWKV_HEAD
    mean = _mm_exact_rhs(y, e2) * inv_n
    yc = y - mean
    var = _mm_exact_rhs(yc * yc, e2) * inv_n
    yn = yc * lax.rsqrt(var + GN_EPS) * lnw_ref[...] + lnb_ref[...]
    bonus = _mm_exact_rhs(r * k2 * rk_ref[...], e2) * v
    o_ref[...] = ((yn + bonus) * g).astype(o_ref.dtype)


def rwkv_branch(p_a, bsz, seq, mu, w0, w2, a0, a2, g2, k_k, k_a, r_k, ln_w, ln_b, tm, col0_lora):
    t = p_a.shape[0]
    c = w0.shape[0]
    nj = c // LANES
    nt = seq // tm
    lw = LORA_PAD
    lora_blk = col0_lora // lw
    assert col0_lora % lw == 0 and seq % tm == 0 and tm % CHUNK == 0

    mu_rkv = mu[:3 * c].reshape(1, 3 * c)
    mu_lo = jnp.pad(mu[3 * c:], (0, lw - (LORA_W + LORA_A + LORA_G))).reshape(1, lw)
    w2p = jnp.pad(w2, ((0, LANES - LORA_W), (0, 0))).astype(BF16)
    a2p = jnp.pad(a2, ((LORA_W, LANES - LORA_W - LORA_A), (0, 0))).astype(BF16)
    g2p = jnp.pad(g2, ((0, 2 * LANES - LORA_G), (0, 0))).astype(BF16)
    vec = lambda z: z.reshape(1, c).astype(F32)

    rpb = tm // SUBLANES

    def cur(off):
        return pl.BlockSpec((tm, LANES), lambda b, j, i: (b * nt + i, off * nj + j))

    def prev(off):
        return pl.BlockSpec((SUBLANES, LANES),
                            lambda b, j, i: (jnp.maximum((b * nt + i) * rpb - 1, 0), off * nj + j))

    def slab(rows):
        return pl.BlockSpec((rows, LANES), lambda b, j, i: (0, j))

    in_specs = [
        cur(0), cur(1), cur(2),
        pl.BlockSpec((tm, lw), lambda b, j, i: (b * nt + i, lora_blk)),
        prev(0), prev(1), prev(2),
        pl.BlockSpec((SUBLANES, lw), lambda b, j, i: (jnp.maximum((b * nt + i) * rpb - 1, 0), lora_blk)),
        pl.BlockSpec((1, LANES), lambda b, j, i: (0, j)),
        pl.BlockSpec((1, LANES), lambda b, j, i: (0, nj + j)),
        pl.BlockSpec((1, LANES), lambda b, j, i: (0, 2 * nj + j)),
        pl.BlockSpec((1, lw), lambda b, j, i: (0, 0)),
        slab(1), slab(1), slab(1), slab(1), slab(1), slab(1), slab(1),
        slab(LANES), slab(LANES), slab(2 * LANES),
    ]
    return pl.pallas_call(
        functools.partial(_rwkv_kernel, tm=tm),
        grid=(bsz, nj, nt),
        in_specs=in_specs,
        out_specs=pl.BlockSpec((tm, LANES), lambda b, j, i: (b * nt + i, j)),
        out_shape=jax.ShapeDtypeStruct((t, c), BF16),
        scratch_shapes=[pltpu.VMEM((2 * CHUNK, 2 * CHUNK), F32)],
        compiler_params=_cp(("parallel", "parallel", "arbitrary")),
        name="rwkv",
    )(p_a, p_a, p_a, p_a, p_a, p_a, p_a, p_a,
      mu_rkv, mu_rkv, mu_rkv, mu_lo, vec(w0), vec(a0), vec(k_k), vec(k_a), vec(r_k), vec(ln_w), vec(ln_b),
      w2p, a2p, g2p)


def _lru_kernel(x_ref, xp_ref, y_ref, cw_ref, cb_ref, wr_ref, br_ref, wi_ref, bi_ref, sp_ref,
                o_ref, h_ref, *, tm, nq):
    i = pl.program_id(1)
    first = i == 0

    @pl.when(first)
    def _():
        h_ref[...] = jnp.zeros_like(h_ref)

    x = x_ref[...]
    prev = jnp.where(first, 0.0, xp_ref[...])
    xx = jnp.concatenate([prev, x], axis=0)
    xc = cb_ref[...] + x * cw_ref[0:1, :]
    for j in range(1, CONV_W):
        xc = xc + xx[SUBLANES - j:SUBLANES - j + tm, :] * cw_ref[j:j + 1, :]

    xcb = xc.astype(BF16)
    qw = x.shape[1] // nq
    gr = jnp.concatenate([_mm(xcb[:, q * qw:(q + 1) * qw], wr_ref[q]) for q in range(nq)], axis=1)
    gi = jnp.concatenate([_mm(xcb[:, q * qw:(q + 1) * qw], wi_ref[q]) for q in range(nq)], axis=1)
    gate_r = _sigmoid(gr + br_ref[...])
    gate_i = _sigmoid(gi + bi_ref[...])
    log_a = -RG_C * gate_r * sp_ref[...]
    a = jnp.exp(log_a)
    mult = jnp.sqrt(jnp.maximum(1.0 - jnp.exp(2.0 * log_a), 0.0))
    row = lax.broadcasted_iota(jnp.int32, x.shape, 0)
    mult = jnp.where(first & (row == 0), 1.0, mult)
    b = mult * gate_i * xc

    sft = 1
    while sft < tm:
        keep = row >= sft
        a_sh = jnp.where(keep, pltpu.roll(a, sft, axis=0), 1.0)
        b_sh = jnp.where(keep, pltpu.roll(b, sft, axis=0), 0.0)
        b = b + a * b_sh
        a = a * a_sh
        sft *= 2
    h = a * h_ref[...] + b
    h_ref[...] = h[tm - 1:tm, :]

    py = y_ref[...]
    gelu = 0.5 * py * (1.0 + jnp.tanh(0.7978845608028654 * (py + 0.044715 * py * py * py)))
    o_ref[...] = (h * gelu).astype(o_ref.dtype)


def lru_branch(p_a, bsz, seq, col_x, col_y, conv_w, conv_b, w_r, b_r, w_i, b_i, lam, tm):
    t = p_a.shape[0]
    c = conv_b.shape[0]
    nt = seq // tm
    rpb = tm // SUBLANES
    bx, by = col_x // c, col_y // c
    assert col_x % c == 0 and col_y % c == 0
    nblk, bw = w_r.shape[0], w_r.shape[1]
    per = max(1, (2 * LANES) // bw)
    nq = nblk // per

    def blockdiag(w):
        w = w.reshape(nq, per, bw, bw)
        eye = jnp.eye(per, dtype=w.dtype)
        return jnp.einsum('qpab,pr->qparb', w, eye).reshape(nq, per * bw, per * bw).astype(BF16)

    vec = lambda z: z.reshape(1, c).astype(F32)
    sp = jax.nn.softplus(-lam)
    return pl.pallas_call(
        functools.partial(_lru_kernel, tm=tm, nq=nq),
        grid=(bsz, nt),
        in_specs=[
            pl.BlockSpec((tm, c), lambda b, i: (b * nt + i, bx)),
            pl.BlockSpec((SUBLANES, c), lambda b, i: (jnp.maximum((b * nt + i) * rpb - 1, 0), bx)),
            pl.BlockSpec((tm, c), lambda b, i: (b * nt + i, by)),
            pl.BlockSpec((CONV_W, c), lambda b, i: (0, 0)),
            pl.BlockSpec((1, c), lambda b, i: (0, 0)),
            pl.BlockSpec((nq, per * bw, per * bw), lambda b, i: (0, 0, 0)),
            pl.BlockSpec((1, c), lambda b, i: (0, 0)),
            pl.BlockSpec((nq, per * bw, per * bw), lambda b, i: (0, 0, 0)),
            pl.BlockSpec((1, c), lambda b, i: (0, 0)),
            pl.BlockSpec((1, c), lambda b, i: (0, 0)),
        ],
        out_specs=pl.BlockSpec((tm, c), lambda b, i: (b * nt + i, 0)),
        out_shape=jax.ShapeDtypeStruct((t, c), BF16),
        scratch_shapes=[pltpu.VMEM((1, c), F32)],
        compiler_params=_cp(("parallel", "arbitrary")),
        name="lru",
    )(p_a, p_a, p_a, conv_w.astype(F32), vec(conv_b), blockdiag(w_r), vec(b_r), blockdiag(w_i), vec(b_i),
      vec(sp))


def _xattn_kernel(q_ref, k_ref, v_ref, o_ref, *, hd):
    scale = hd ** -0.5
    outs = []
    for h in range(XA_HEADS):
        sl = slice(h * hd, (h + 1) * hd)
        s = _mm_nt(q_ref[:, sl], k_ref[:, sl]) * scale
        m = jnp.max(s, axis=-1, keepdims=True)
        e = jnp.exp(s - m)
        pr = e / jnp.sum(e, axis=-1, keepdims=True)
        outs.append(_mm(pr.astype(BF16), v_ref[:, sl]))
    o_ref[...] = jnp.concatenate(outs, axis=1).astype(o_ref.dtype)


def xattn_branch(p_b, kv, bsz, seq, tm):
    t = p_b.shape[0]
    c = kv.shape[1] // 2
    mlen = kv.shape[0] // bsz
    nt = seq // tm
    return pl.pallas_call(
        functools.partial(_xattn_kernel, hd=c // XA_HEADS),
        grid=(bsz, nt),
        in_specs=[
            pl.BlockSpec((tm, c), lambda b, i: (b * nt + i, 0)),
            pl.BlockSpec((mlen, c), lambda b, i: (b, 0)),
            pl.BlockSpec((mlen, c), lambda b, i: (b, 1)),
        ],
        out_specs=pl.BlockSpec((tm, c), lambda b, i: (b * nt + i, 0)),
        out_shape=jax.ShapeDtypeStruct((t, c), BF16),
        compiler_params=_cp(("parallel", "parallel")),
        name="xattn",
    )(p_b, kv, kv)


def _merge_kernel(ya_ref, yb_ref, yc_ref, wa_ref, wb_ref, wc_ref, ga_ref, gb_ref, gc_ref, o_ref):
    acc = _sigmoid(ga_ref[...].astype(F32)) * _mm(ya_ref[...], wa_ref[...])
    acc = acc + _sigmoid(gb_ref[...].astype(F32)) * _mm(yb_ref[...], wb_ref[...])
    acc = acc + _sigmoid(gc_ref[...].astype(F32)) * _mm(yc_ref[...], wc_ref[...])
    o_ref[...] = acc.astype(o_ref.dtype)


def merge(y_a, y_b, y_c, w_a, w_b, w_c, p_b, gate_col0, tm, tn):
    t, c = y_a.shape
    d = w_a.shape[1]
    g0 = gate_col0 // tn
    gstep = d // tn
    assert gate_col0 % tn == 0
    yspec = pl.BlockSpec((tm, c), lambda i, j: (i, 0))
    wspec = pl.BlockSpec((c, tn), lambda i, j: (0, j))
    gspec = lambda n: pl.BlockSpec((tm, tn), lambda i, j: (i, g0 + n * gstep + j))
    return pl.pallas_call(
        _merge_kernel,
        grid=(t // tm, d // tn),
        in_specs=[yspec, yspec, yspec, wspec, wspec, wspec, gspec(0), gspec(1), gspec(2)],
        out_specs=pl.BlockSpec((tm, tn), lambda i, j: (i, j)),
        out_shape=jax.ShapeDtypeStruct((t, d), BF16),
        compiler_params=_cp(("parallel", "parallel")),
        name="merge",
    )(y_a, y_b, y_c, w_a, w_b, w_c, p_b, p_b, p_b)


def _wo_kernel(x_ref, m_ref, wo_ref, g_ref, wrh_ref, wrl_ref, br_ref, h_ref, xn_ref, lg_ref):
    h = x_ref[...].astype(F32) + _mm(m_ref[...], wo_ref[...])
    h_ref[...] = h
    ms = jnp.mean(h * h, axis=-1, keepdims=True)
    xn = h * lax.rsqrt(ms + EPS) * g_ref[...]
    xn_ref[...] = xn
    hi, lo = _split(xn)
    lg_ref[...] = _mm(hi, wrh_ref[...]) + _mm(lo, wrh_ref[...]) + _mm(hi, wrl_ref[...]) + br_ref[...]


def wo_block(x, merged, w_o, g_ffn, w_router, b_router, tm):
    t, d = x.shape
    ne = w_router.shape[1]
    wr = jnp.pad(w_router.astype(F32), ((0, 0), (0, LANES - ne)))
    wr_hi = wr.astype(BF16)
    wr_lo = (wr - wr_hi.astype(F32)).astype(BF16)
    br = jnp.pad(b_router.astype(F32), (0, LANES - ne), constant_values=NEG_BIG).reshape(1, LANES)
    row = pl.BlockSpec((tm, d), lambda i: (i, 0))
    full = lambda r, c: pl.BlockSpec((r, c), lambda i: (0, 0))
    return pl.pallas_call(
        _wo_kernel,
        grid=(t // tm,),
        in_specs=[row, row, full(d, d), full(1, d), full(d, LANES), full(d, LANES), full(1, LANES)],
        out_specs=[row, row, pl.BlockSpec((tm, LANES), lambda i: (i, 0))],
        out_shape=[jax.ShapeDtypeStruct((t, d), F32), jax.ShapeDtypeStruct((t, d), F32),
                   jax.ShapeDtypeStruct((t, LANES), F32)],
        compiler_params=_cp(("parallel",)),
        name="wo",
    )(x, merged, w_o.astype(BF16), g_ffn.reshape(1, d).astype(F32), wr_hi, wr_lo, br)


def _router_kernel(lg_ref, idx_ref, gate_ref, rank_ref, cnt_ref, carry_ref, *, tm):
    i = pl.program_id(0)

    @pl.when(i == 0)
    def _():
        carry_ref[...] = jnp.zeros_like(carry_ref)

    l = lg_ref[...]
    lane = lax.broadcasted_iota(jnp.int32, l.shape, 1).astype(F32)
    vals, sels, idxs = [], [], []
    onehot = jnp.zeros(l.shape, F32)
    for _ in range(TOP_K):
        m = jnp.max(l, axis=-1, keepdims=True)
        idx = jnp.min(jnp.where(l == m, lane, float(LANES)), axis=-1, keepdims=True)
        sel = lane == idx
        vals.append(m)
        sels.append(sel)
        idxs.append(idx)
        onehot = onehot + jnp.where(sel, 1.0, 0.0)
        l = jnp.where(sel, -jnp.inf, l)
    es = [jnp.exp(vv - vals[0]) for vv in vals]
    den = es[0] + es[1] + es[2] + es[3]

    trow = lax.broadcasted_iota(jnp.int32, (tm, tm), 0)
    tcol = lax.broadcasted_iota(jnp.int32, (tm, tm), 1)
    tri = jnp.where(tcol < trow, 1.0, 0.0).astype(BF16)
    cum = _mm(tri, onehot.astype(BF16)) + carry_ref[...]
    carry = carry_ref[...] + jnp.sum(onehot, axis=0, keepdims=True)
    carry_ref[...] = carry
    cnt_ref[...] = carry

    idx_out = jnp.zeros(l.shape, F32)
    gate_out = jnp.zeros(l.shape, F32)
    rank_out = jnp.zeros(l.shape, F32)
    for kk in range(TOP_K):
        rk = jnp.sum(jnp.where(sels[kk], cum, 0.0), axis=-1, keepdims=True)
        idx_out = jnp.where(lane == kk, idxs[kk], idx_out)
        gate_out = jnp.where(lane == kk, es[kk] / den, gate_out)
        rank_out = jnp.where(lane == kk, rk, rank_out)
    idx_ref[...] = idx_out.astype(jnp.int32)
    gate_ref[...] = gate_out
    rank_ref[...] = rank_out.astype(jnp.int32)


def router(logits, tm):
    t = logits.shape[0]
    spec = pl.BlockSpec((tm, LANES), lambda i: (i, 0))
    return pl.pallas_call(
        functools.partial(_router_kernel, tm=tm),
        grid=(t // tm,),
        in_specs=[spec],
        out_specs=[spec, spec, spec, pl.BlockSpec((1, LANES), lambda i: (0, 0))],
        out_shape=[jax.ShapeDtypeStruct((t, LANES), jnp.int32), jax.ShapeDtypeStruct((t, LANES), F32),
                   jax.ShapeDtypeStruct((t, LANES), jnp.int32), jax.ShapeDtypeStruct((1, LANES), F32)],
        scratch_shapes=[pltpu.VMEM((1, LANES), F32)],
        compiler_params=_cp(("arbitrary",)),
        name="router",
    )(logits)


def _expert_kernel(be_ref, nu_ref, tok_ref, x_ref, wg_ref, bg_ref, wu_ref, bu_ref, wd_ref, bd_ref,
                   o_ref, idx_ref, xf_ref, xb_ref, acc_ref, isem, gsem, *, rows):
    b = pl.program_id(0)
    j = pl.program_id(1)
    nj = pl.num_programs(1)
    used = b < nu_ref[0]

    @pl.when(used & (j == 0))
    def _():
        icp = pltpu.make_async_copy(tok_ref.at[pl.ds(pl.multiple_of(b * IDX_PAD, IDX_PAD), IDX_PAD)],
                                    idx_ref, isem)
        icp.start()
        icp.wait()

        def issue(r, carry):
            pltpu.make_async_copy(x_ref.at[pl.ds(idx_ref[r], 1), :], xf_ref.at[pl.ds(r, 1), :], gsem).start()
            return carry

        lax.fori_loop(0, rows, issue, 0)
        pltpu.make_async_copy(x_ref.at[pl.ds(0, rows), :], xf_ref, gsem).wait()
        xb_ref[...] = xf_ref[...].astype(BF16)
        acc_ref[...] = jnp.zeros_like(acc_ref)

    @pl.when(used)
    def _():
        xb = xb_ref[...]
        hg = jnp.minimum(_mm(xb, wg_ref[...].astype(BF16)) + bg_ref[...], SWIGLU_LIMIT)
        hu = jnp.clip(_mm(xb, wu_ref[...].astype(BF16)) + bu_ref[...], -SWIGLU_LIMIT, SWIGLU_LIMIT)
        hh = (hu + 1.0) * hg * _sigmoid(SWIGLU_ALPHA * hg)
        acc_ref[...] += _mm(hh.astype(BF16), wd_ref[...].astype(BF16))

    @pl.when(used & (j == nj - 1))
    def _():
        o_ref[...] = acc_ref[...] + bd_ref[...]

    @pl.when(jnp.logical_not(used) & (j == nj - 1))
    def _():
        o_ref[...] = jnp.zeros_like(o_ref)


def experts(xn2, tok_sorted, block_e, n_used, w_gate, b_gate, w_up, b_up, w_down, b_down, tn):
    t, d = xn2.shape
    ne, _, de = w_gate.shape
    n_blocks = tok_sorted.shape[0] // IDX_PAD
    n_rows = n_blocks * ROW_BLOCK
    nj = de // tn

    def eb(b, be, nu):
        return be[jnp.minimum(b, nu[0] - 1)]

    def jb(b, j, nu):
        return jnp.where(b < nu[0], j, nj - 1)

    grid_spec = pltpu.PrefetchScalarGridSpec(
        num_scalar_prefetch=2,
        grid=(n_blocks, nj),
        in_specs=[
            pl.BlockSpec(memory_space=pl.ANY),
            pl.BlockSpec(memory_space=pl.ANY),
            pl.BlockSpec((None, d, tn), lambda b, j, be, nu: (eb(b, be, nu), 0, jb(b, j, nu))),
            pl.BlockSpec((None, 1, tn), lambda b, j, be, nu: (eb(b, be, nu), 0, jb(b, j, nu))),
            pl.BlockSpec((None, d, tn), lambda b, j, be, nu: (eb(b, be, nu), 0, jb(b, j, nu))),
            pl.BlockSpec((None, 1, tn), lambda b, j, be, nu: (eb(b, be, nu), 0, jb(b, j, nu))),
            pl.BlockSpec((None, tn, d), lambda b, j, be, nu: (eb(b, be, nu), jb(b, j, nu), 0)),
            pl.BlockSpec((None, 1, d), lambda b, j, be, nu: (eb(b, be, nu), 0, 0)),
        ],
        out_specs=pl.BlockSpec((ROW_BLOCK, d), lambda b, j, be, nu: (b, 0)),
        scratch_shapes=[
            pltpu.SMEM((IDX_PAD,), jnp.int32),
            pltpu.VMEM((ROW_BLOCK, d), F32),
            pltpu.VMEM((ROW_BLOCK, d), BF16),
            pltpu.VMEM((ROW_BLOCK, d), F32),
            pltpu.SemaphoreType.DMA,
            pltpu.SemaphoreType.DMA,
        ],
    )
    return pl.pallas_call(
        functools.partial(_expert_kernel, rows=ROW_BLOCK),
        grid_spec=grid_spec,
        out_shape=jax.ShapeDtypeStruct((n_rows, d), F32),
        compiler_params=_cp(("arbitrary", "arbitrary")),
        name="experts",
    )(block_e, n_used, tok_sorted, xn2, w_gate, b_gate.reshape(ne, 1, de), w_up, b_up.reshape(ne, 1, de),
      w_down, b_down.reshape(ne, 1, d))


def _combine_kernel(dest_ref, rows_ref, h_ref, gate_ref, g_ref, o_ref, idx_ref, buf_ref, isem, gsem, *, tm):
    i = pl.program_id(0)
    n = tm * TOP_K
    icp = pltpu.make_async_copy(dest_ref.at[pl.ds(pl.multiple_of(i * n, IDX_PAD), n)], idx_ref, isem)
    icp.start()
    icp.wait()

    def issue(r, carry):
        pltpu.make_async_copy(rows_ref.at[pl.ds(idx_ref[r], 1), :], buf_ref.at[pl.ds(r, 1), :], gsem).start()
        return carry

    lax.fori_loop(0, n, issue, 0)
    pltpu.make_async_copy(rows_ref.at[pl.ds(0, n), :], buf_ref, gsem).wait()

    gates = gate_ref[...]
    h = h_ref[...]
    for kk in range(TOP_K):
        h = h + gates[:, kk:kk + 1] * buf_ref[kk * tm:(kk + 1) * tm, :]
    ms = jnp.mean(h * h, axis=-1, keepdims=True)
    o_ref[...] = (h * lax.rsqrt(ms + EPS) * g_ref[...]).astype(o_ref.dtype)


def combine(dest_km, out_rows, h1, gates, g_final, tm, out_dtype):
    t, d = h1.shape
    return pl.pallas_call(
        functools.partial(_combine_kernel, tm=tm),
        grid=(t // tm,),
        in_specs=[
            pl.BlockSpec(memory_space=pl.ANY),
            pl.BlockSpec(memory_space=pl.ANY),
            pl.BlockSpec((tm, d), lambda i: (i, 0)),
            pl.BlockSpec((tm, LANES), lambda i: (i, 0)),
            pl.BlockSpec((1, d), lambda i: (0, 0)),
        ],
        out_specs=pl.BlockSpec((tm, d), lambda i: (i, 0)),
        out_shape=jax.ShapeDtypeStruct((t, d), out_dtype),
        scratch_shapes=[
            pltpu.SMEM((tm * TOP_K,), jnp.int32),
            pltpu.VMEM((tm * TOP_K, d), F32),
            pltpu.SemaphoreType.DMA,
            pltpu.SemaphoreType.DMA,
        ],
        compiler_params=_cp(("arbitrary",)),
        name="combine",
    )(dest_km, out_rows, h1, gates, g_final.reshape(1, d).astype(F32))


def _tile(n, pref):
    while n % pref:
        pref //= 2
    return pref


def kernel(x, mem, g_mix, w_in, mu_rwkv, rwkv_w0, rwkv_w2, rwkv_a0, rwkv_a2, rwkv_g2, rwkv_k_k, rwkv_k_a,
           rwkv_r_k, rwkv_ln_w, rwkv_ln_b, w_rwkv_out, lru_conv_w, lru_conv_b, lru_w_r, lru_b_r, lru_w_i,
           lru_b_i, lru_lambda, w_lru_out, g_mem, w_mem_k, w_mem_v, w_xa_out, w_o, g_ffn, w_router,
           b_router, w_gate, b_gate, w_up, b_up, w_down, b_down, g_final):
    bsz, seq, d = x.shape
    assert w_in.shape[0] == 1, "single-layer block"
    l = 0
    t = bsz * seq
    c = rwkv_w0.shape[1]
    n_lora = LORA_W + LORA_A + LORA_G
    ne = w_router.shape[2]
    assert lru_conv_b.shape[1] == c and w_mem_k.shape[2] == c and c % LORA_PAD == 0

    h = x.reshape(t, d).astype(F32)
    memf = mem.reshape(bsz * mem.shape[1], d)
    if True:
        wi = w_in[l]
        c0 = 3 * c + n_lora
        w_a = jnp.concatenate(
            [wi[:, 0:3 * c], wi[:, c0:c0 + 2 * c], wi[:, 3 * c:c0], jnp.zeros((d, LORA_PAD - n_lora), wi.dtype)],
            axis=1).astype(BF16)
        w_b = wi[:, c0 + 2 * c:].astype(BF16)

        xn = norm_cast(h, g_mix[l], _tile(t, 512))
        p_a = matmul(xn, w_a, _tile(t, 1024), _tile(w_a.shape[1], 512), F32)
        p_b = matmul(xn, w_b, _tile(t, 1024), _tile(w_b.shape[1], 512), BF16)
        mn = norm_cast(memf, g_mem[l], _tile(memf.shape[0], 256))
        kv = matmul(mn, jnp.concatenate([w_mem_k[l], w_mem_v[l]], axis=1).astype(BF16),
                    _tile(memf.shape[0], 256), _tile(2 * c, 512), BF16)

        y_a = rwkv_branch(p_a, bsz, seq, mu_rwkv[l], rwkv_w0[l], rwkv_w2[l], rwkv_a0[l], rwkv_a2[l],
                          rwkv_g2[l], rwkv_k_k[l], rwkv_k_a[l], rwkv_r_k[l].reshape(-1), rwkv_ln_w[l],
                          rwkv_ln_b[l], _tile(seq, 256), 5 * c)
        y_b = lru_branch(p_a, bsz, seq, 3 * c, 4 * c, lru_conv_w[l], lru_conv_b[l], lru_w_r[l], lru_b_r[l],
                         lru_w_i[l], lru_b_i[l], lru_lambda[l], _tile(seq, 256))
        y_c = xattn_branch(p_b, kv, bsz, seq, _tile(seq, 512))
        merged = merge(y_a, y_b, y_c, w_rwkv_out[l].astype(BF16), w_lru_out[l].astype(BF16),
                       w_xa_out[l].astype(BF16), p_b, c, _tile(t, 512), _tile(d, 512))
        h1, xn2, logits = wo_block(h, merged, w_o[l], g_ffn[l], w_router[l], b_router[l], _tile(t, 256))

        idx, gates, rank, counts = router(logits, _tile(t, 512))
        e_tk = idx[:, :TOP_K]
        counts = counts[0, :ne].astype(jnp.int32)
        padded = ((counts + ROW_BLOCK - 1) // ROW_BLOCK) * ROW_BLOCK
        pend = jnp.cumsum(padded)
        pstart = pend - padded
        dest = pstart[e_tk] + rank[:, :TOP_K]
        n_rows = (-(-(t * TOP_K) // ROW_BLOCK)) * ROW_BLOCK + ne * ROW_BLOCK
        n_blocks = n_rows // ROW_BLOCK
        tok = jnp.broadcast_to(jnp.arange(t, dtype=jnp.int32)[:, None], (t, TOP_K))
        tok_sorted = jnp.zeros((n_rows,), jnp.int32).at[dest.reshape(-1)].set(tok.reshape(-1))
        tok_sorted = jnp.pad(tok_sorted.reshape(n_blocks, ROW_BLOCK), ((0, 0), (0, IDX_PAD - ROW_BLOCK)))
        block_start = jnp.arange(n_blocks, dtype=jnp.int32) * ROW_BLOCK
        block_e = jnp.minimum(jnp.searchsorted(pend, block_start, side='right'), ne - 1).astype(jnp.int32)
        n_used = (pend[-1] // ROW_BLOCK).astype(jnp.int32).reshape(1)

        out_rows = experts(xn2, tok_sorted.reshape(-1), block_e, n_used, w_gate[l], b_gate[l], w_up[l],
                           b_up[l], w_down[l], b_down[l], _tile(w_gate.shape[3], 256))
        tmc = IDX_PAD // TOP_K
        assert t % tmc == 0
        dest_km = dest.reshape(t // tmc, tmc, TOP_K).transpose(0, 2, 1).reshape(-1).astype(jnp.int32)
        out = combine(dest_km, out_rows, h1, gates, g_final, tmc, x.dtype)
    return out.reshape(bsz, seq, d)
```

```python
import functools

import jax
import jax.numpy as jnp
from jax import lax
from jax.experimental import pallas as pl
from jax.experimental.pallas import tpu as pltpu

F32 = jnp.float32
BF16 = jnp.bfloat16

EPS = 1e-6
GN_EPS = 64e-5
RG_C = 8.0
SWIGLU_LIMIT = 7.0
SWIGLU_ALPHA = 1.702
RWKV_HEAD = 64
LORA_W = 64
LORA_A = 64
LORA_G = 160
CONV_W = 4
LRU_BLOCKS = 16
XA_HEADS = 4
TOP_K = 4
N_BRANCH = 3

LANES = 128
SUBLANES = 8
CHUNK = 64
SUB = 16
SUB_ROWS = 256
GROUP_SUBS = 4
LORA_PAD = 512
NEG_BIG = -1e30
VMEM_LIMIT = 56 * 1024 * 1024


def _cp(sem, vmem=None):
    return pltpu.CompilerParams(dimension_semantics=sem, vmem_limit_bytes=vmem or VMEM_LIMIT)


def _mm(a, b):
    return jnp.dot(a, b, preferred_element_type=F32)


def _mm_nt(a, b):
    return lax.dot_general(a, b, (((1,), (1,)), ((), ())), preferred_element_type=F32)


def _mm_tn(a, b):
    return lax.dot_general(a, b, (((0,), (0,)), ((), ())), preferred_element_type=F32)


def _split(x):
    hi = x.astype(BF16)
    lo = (x - hi.astype(F32)).astype(BF16)
    return hi, lo


def _mm_exact_rhs(x, m):
    hi, lo = _split(x)
    return _mm(hi, m) + _mm(lo, m)


def _mm_exact_lhs(m, x):
    hi, lo = _split(x)
    return _mm(m, hi) + _mm(m, lo)


def _sigmoid(x):
    return 1.0 / (1.0 + jnp.exp(-x))


def _softplus(x):
    return jnp.maximum(x, 0.0) + jnp.log(1.0 + jnp.exp(-jnp.abs(x)))


def _norm_kernel(x_ref, g_ref, o_ref):
    x = x_ref[...].astype(F32)
    ms = jnp.mean(x * x, axis=-1, keepdims=True)
    o_ref[...] = (x * lax.rsqrt(ms + EPS) * g_ref[...]).astype(o_ref.dtype)


def norm_cast(x, g, tm, out_dtype=BF16):
    t, d = x.shape
    return pl.pallas_call(
        _norm_kernel,
        grid=(t // tm,),
        in_specs=[pl.BlockSpec((tm, d), lambda i: (i, 0)), pl.BlockSpec((1, d), lambda i: (0, 0))],
        out_specs=pl.BlockSpec((tm, d), lambda i: (i, 0)),
        out_shape=jax.ShapeDtypeStruct((t, d), out_dtype),
        compiler_params=_cp(("parallel",)),
        name="norm_cast",
    )(x, g.reshape(1, d).astype(F32))


def _matmul_kernel(a_ref, w_ref, o_ref):
    o_ref[...] = _mm(a_ref[...], w_ref[...]).astype(o_ref.dtype)


def matmul(a, w, tm, tn, out_dtype):
    m, k = a.shape
    n = w.shape[1]
    return pl.pallas_call(
        _matmul_kernel,
        grid=(m // tm, n // tn),
        in_specs=[pl.BlockSpec((tm, k), lambda i, j: (i, 0)), pl.BlockSpec((k, tn), lambda i, j: (0, j))],
        out_specs=pl.BlockSpec((tm, tn), lambda i, j: (i, j)),
        out_shape=jax.ShapeDtypeStruct((m, n), out_dtype),
        compiler_params=_cp(("parallel", "parallel")),
        name="matmul",
    )(a, w)


def _rwkv_kernel(r_ref, k_ref, v_ref, lo_ref, rp_ref, kp_ref, vp_ref, lop_ref,
                 mur_ref, muk_ref, muv_ref, mulo_ref, w0_ref, a0_ref, kk_ref, ka_ref, rk_ref,
                 lnw_ref, lnb_ref, w2_ref, a2_ref, g2_ref, o_ref, s_ref, *, tm, nb):
    i = pl.program_id(1)
    first = i == 0

    @pl.when(first)
    def _():
        s_ref[...] = jnp.zeros_like(s_ref)

    bf = lambda z: z.astype(BF16)
    bs = range(nb)

    def mix(x_ref, p_ref, mu_ref, b):
        x = x_ref[b]
        prev = jnp.where(first, 0.0, p_ref[b, SUBLANES - 1:SUBLANES, :])
        row = lax.broadcasted_iota(jnp.int32, x.shape, 0)
        sh = jnp.where(row == 0, prev, pltpu.roll(x, 1, axis=0))
        return x + (sh - x) * mu_ref[...]

    r = [mix(r_ref, rp_ref, mur_ref, b) for b in bs]
    k = [mix(k_ref, kp_ref, muk_ref, b) for b in bs]
    v = [mix(v_ref, vp_ref, muv_ref, b) for b in bs]
    lo = [mix(lo_ref, lop_ref, mulo_ref, b) for b in bs]

    wl = [_mm(bf(jnp.tanh(z[:, 0:LANES])), w2_ref[...]) for z in lo]
    al = [_mm(bf(z[:, 0:LANES]), a2_ref[...]) for z in lo]
    g = [_mm(bf(_sigmoid(z[:, LANES:3 * LANES])), g2_ref[...]) for z in lo]

    ld = [-jnp.exp(-_softplus(-(w0_ref[...] + z)) - 0.5) for z in wl]
    a_sig = [_sigmoid(a0_ref[...] + z) for z in al]

    lane = lax.broadcasted_iota(jnp.int32, (LANES, LANES), 1)
    sub = lax.broadcasted_iota(jnp.int32, (LANES, LANES), 0)
    e2 = jnp.where((lane // RWKV_HEAD) == (sub // RWKV_HEAD), 1.0, 0.0).astype(BF16)

    kk = [z * kk_ref[...] for z in k]
    kk = [z * lax.rsqrt(jnp.maximum(_mm_exact_rhs(z * z, e2), 1e-24)) for z in kk]
    k2 = [k[b] * (1.0 + (a_sig[b] - 1.0) * ka_ref[...]) for b in bs]
    a_vec = [-z for z in kk]
    b_vec = [kk[b] * a_sig[b] for b in bs]

    trow = lax.broadcasted_iota(jnp.int32, (tm, tm), 0)
    tcol = lax.broadcasted_iota(jnp.int32, (tm, tm), 1)
    tri = jnp.where(((trow // CHUNK) == (tcol // CHUNK)) & (tcol <= trow), 1.0, 0.0).astype(BF16)
    c = [_mm_exact_lhs(tri, z) for z in ld]

    lane_row = lax.broadcasted_iota(jnp.int32, (CHUNK, LANES), 1)
    head_a = lane_row < RWKV_HEAD

    def stack(x):
        return jnp.concatenate([jnp.where(head_a, x, 0.0), jnp.where(head_a, 0.0, x)], axis=0)

    n2 = 2 * CHUNK
    ri = lax.broadcasted_iota(jnp.int32, (n2, n2), 0)
    ci = lax.broadcasted_iota(jnp.int32, (n2, n2), 1)
    strict = ri > ci
    incl = ri >= ci
    blk = (ri // SUB) == (ci // SUB)
    eye = jnp.where(ri == ci, 1.0, 0.0)

    nch = tm // CHUNK
    items = [(b, q) for q in range(nch) for b in bs]
    rng = range(len(items))
    sl = lambda q: slice(q * CHUNK, (q + 1) * CHUNK)
    cs = [c[b][sl(q)] for b, q in items]
    lds = [ld[b][sl(q)] for b, q in items]
    c_ends = [z[CHUNK - 1:CHUNK, :] for z in cs]
    e_pos = [jnp.exp(z) for z in cs]
    e_neg = [jnp.exp(-z) for z in cs]
    e_end = [jnp.exp(ce - z) for ce, z in zip(c_ends, cs)]
    p_end = [jnp.exp(ce) for ce in c_ends]
    rt32 = [stack(r[b][sl(q)] * e_pos[n]) for n, (b, q) in enumerate(items)]
    at = [bf(stack(a_vec[b][sl(q)] * jnp.exp(cs[n] - lds[n]))) for n, (b, q) in enumerate(items)]
    rt = [bf(z) for z in rt32]
    bt = [bf(stack(b_vec[b][sl(q)] * e_neg[n])) for n, (b, q) in enumerate(items)]
    kt = [bf(stack(k2[b][sl(q)] * e_neg[n])) for n, (b, q) in enumerate(items)]
    bh = [bf(stack(b_vec[b][sl(q)] * e_end[n])) for n, (b, q) in enumerate(items)]
    kh = [bf(stack(k2[b][sl(q)] * e_end[n])) for n, (b, q) in enumerate(items)]
    vs = [bf(stack(v[b][sl(q)])) for b, q in items]

    gram = [_mm_nt(jnp.concatenate([at[n], rt[n]], axis=0), jnp.concatenate([bt[n], kt[n]], axis=0))
            for n in rng]
    l_ab = [jnp.where(strict, z[0:n2, 0:n2], 0.0) for z in gram]
    l_ak = [bf(jnp.where(strict, z[0:n2, n2:], 0.0)) for z in gram]
    m_rb = [bf(jnp.where(incl, z[n2:, 0:n2], 0.0)) for z in gram]
    m_rk = [bf(jnp.where(incl, z[n2:, n2:], 0.0)) for z in gram]

    d = [jnp.where(blk, z, 0.0) for z in l_ab]
    l_off = [bf(l_ab[n] - d[n]) for n in rng]
    db = [bf(z) for z in d]
    p = [eye + z for z in d]
    d2 = [bf(_mm(z, z)) for z in db]
    lv = [_mm(l_ak[n], vs[n]) for n in rng]
    p = [p[n] + _mm(d2[n], bf(p[n])) for n in rng]
    d4 = [bf(_mm(z, z)) for z in d2]
    p = [p[n] + _mm(d4[n], bf(p[n])) for n in rng]
    d8 = [bf(_mm(z, z)) for z in d4]
    td = [bf(p[n] + _mm(d8[n], bf(p[n]))) for n in rng]
    mm = [bf(_mm(td[n], l_off[n])) for n in rng]
    x = [_mm(td[n], jnp.concatenate([at[n], bf(lv[n])], axis=1)) for n in rng]
    m2 = [bf(_mm(z, z)) for z in mm]
    x = [x[n] + _mm(mm[n], bf(x[n])) for n in rng]
    xb = [bf(x[n] + _mm(m2[n], bf(x[n]))) for n in rng]
    wb = [z[:, 0:n2] for z in xb]
    u0b = [z[:, n2:] for z in xb]

    t1 = [_mm(m_rb[n], xb[n]) for n in rng]
    rp = [bf(rt32[n] + t1[n][:, 0:n2]) for n in rng]
    y0 = [t1[n][:, n2:] + _mm(m_rk[n], vs[n]) for n in rng]
    ab = [bf(_mm_tn(bh[n], wb[n])) for n in rng]
    gt = [_mm_tn(u0b[n], bh[n]) + _mm_tn(vs[n], kh[n]) for n in rng]

    s = [s_ref[b] for b in bs]
    ys = [[] for _ in bs]
    for n, (b, q) in enumerate(items):
        sb = bf(s[b])
        yst = _mm_nt(rp[n], sb) + y0[n]
        s[b] = s[b] * p_end[n] + _mm_nt(sb, ab[n]) + gt[n]
        ys[b].append(yst[0:CHUNK] + yst[CHUNK:n2])
    inv_n = 1.0 / RWKV_HEAD
    for b in bs:
        s_ref[b] = s[b]
    y = [jnp.concatenate(z, axis=0) if len(z) > 1 else z[0] for z in ys]
    mean = [_mm_exact_rhs(z, e2) * inv_n for z in y]
    yc = [y[b] - mean[b] for b in bs]
    var = [_mm_exact_rhs(z * z, e2) * inv_n for z in yc]
    bonus = [_mm_exact_rhs(r[b] * k2[b] * rk_ref[...], e2) * v[b] for b in bs]
    for b in bs:
        yn = yc[b] * lax.rsqrt(var[b] + GN_EPS) * lnw_ref[...] + lnb_ref[...]
        o_ref[b] = ((yn + bonus[b]) * g[b]).astype(o_ref.dtype)


def rwkv_branch(p_a, bsz, seq, mu, w0, w2, a0, a2, g2, k_k, k_a, r_k, ln_w, ln_b, tm, col0_lora):
    t, na = p_a.shape
    c = w0.shape[0]
    nj = c // LANES
    nt = seq // tm
    lw = LORA_PAD
    lora_blk = col0_lora // lw
    assert col0_lora % lw == 0 and seq % tm == 0 and tm % CHUNK == 0
    p3 = p_a.reshape(bsz, seq, na)

    mu_rkv = mu[:3 * c].reshape(1, 3 * c)
    mu_lo = jnp.pad(mu[3 * c:], (0, lw - (LORA_W + LORA_A + LORA_G))).reshape(1, lw)
    w2p = jnp.pad(w2, ((0, LANES - LORA_W), (0, 0))).astype(BF16)
    a2p = jnp.pad(a2, ((LORA_W, LANES - LORA_W - LORA_A), (0, 0))).astype(BF16)
    g2p = jnp.pad(g2, ((0, 2 * LANES - LORA_G), (0, 0))).astype(BF16)
    vec = lambda z: z.reshape(1, c).astype(F32)

    rpb = tm // SUBLANES

    def cur(off):
        return pl.BlockSpec((bsz, tm, LANES), lambda j, i: (0, i, off * nj + j))

    def prev(off):
        return pl.BlockSpec((bsz, SUBLANES, LANES), lambda j, i: (0, jnp.maximum(i * rpb - 1, 0), off * nj + j))

    def slab(rows):
        return pl.BlockSpec((rows, LANES), lambda j, i: (0, j))

    in_specs = [
        cur(0), cur(1), cur(2),
        pl.BlockSpec((bsz, tm, lw), lambda j, i: (0, i, lora_blk)),
        prev(0), prev(1), prev(2),
        pl.BlockSpec((bsz, SUBLANES, lw), lambda j, i: (0, jnp.maximum(i * rpb - 1, 0), lora_blk)),
        pl.BlockSpec((1, LANES), lambda j, i: (0, j)),
        pl.BlockSpec((1, LANES), lambda j, i: (0, nj + j)),
        pl.BlockSpec((1, LANES), lambda j, i: (0, 2 * nj + j)),
        pl.BlockSpec((1, lw), lambda j, i: (0, 0)),
        slab(1), slab(1), slab(1), slab(1), slab(1), slab(1), slab(1),
        slab(LANES), slab(LANES), slab(2 * LANES),
    ]
    out = pl.pallas_call(
        functools.partial(_rwkv_kernel, tm=tm, nb=bsz),
        grid=(nj, nt),
        in_specs=in_specs,
        out_specs=pl.BlockSpec((bsz, tm, LANES), lambda j, i: (0, i, j)),
        out_shape=jax.ShapeDtypeStruct((bsz, seq, c), BF16),
        scratch_shapes=[pltpu.VMEM((bsz, 2 * CHUNK, 2 * CHUNK), F32)],
        compiler_params=_cp(("parallel", "arbitrary")),
        name="rwkv",
    )(p3, p3, p3, p3, p3, p3, p3, p3,
      mu_rkv, mu_rkv, mu_rkv, mu_lo, vec(w0), vec(a0), vec(k_k), vec(k_a), vec(r_k), vec(ln_w), vec(ln_b),
      w2p, a2p, g2p)
    return out.reshape(t, c)


def _lru_kernel(x_ref, xp_ref, y_ref, cw_ref, cb_ref, wr_ref, br_ref, wi_ref, bi_ref, sp_ref,
                o_ref, h_ref, *, tm, nq):
    i = pl.program_id(1)
    first = i == 0

    @pl.when(first)
    def _():
        h_ref[...] = jnp.zeros_like(h_ref)

    x = x_ref[...]
    prev = jnp.where(first, 0.0, xp_ref[...])
    xx = jnp.concatenate([prev, x], axis=0)
    xc = cb_ref[...] + x * cw_ref[0:1, :]
    for j in range(1, CONV_W):
        xc = xc + xx[SUBLANES - j:SUBLANES - j + tm, :] * cw_ref[j:j + 1, :]

    xcb = xc.astype(BF16)
    qw = x.shape[1] // nq
    gr = jnp.concatenate([_mm(xcb[:, q * qw:(q + 1) * qw], wr_ref[q]) for q in range(nq)], axis=1)
    gi = jnp.concatenate([_mm(xcb[:, q * qw:(q + 1) * qw], wi_ref[q]) for q in range(nq)], axis=1)
    gate_r = _sigmoid(gr + br_ref[...])
    gate_i = _sigmoid(gi + bi_ref[...])
    log_a = -RG_C * gate_r * sp_ref[...]
    a = jnp.exp(log_a)
    mult = jnp.sqrt(jnp.maximum(1.0 - jnp.exp(2.0 * log_a), 0.0))
    row = lax.broadcasted_iota(jnp.int32, x.shape, 0)
    mult = jnp.where(first & (row == 0), 1.0, mult)
    b = mult * gate_i * xc

    sft = 1
    while sft < tm:
        keep = row >= sft
        a_sh = jnp.where(keep, pltpu.roll(a, sft, axis=0), 1.0)
        b_sh = jnp.where(keep, pltpu.roll(b, sft, axis=0), 0.0)
        b = b + a * b_sh
        a = a * a_sh
        sft *= 2
    h = a * h_ref[...] + b
    h_ref[...] = h[tm - 1:tm, :]

    py = y_ref[...]
    gelu = 0.5 * py * (1.0 + jnp.tanh(0.7978845608028654 * (py + 0.044715 * py * py * py)))
    o_ref[...] = (h * gelu).astype(o_ref.dtype)


def lru_branch(p_a, bsz, seq, col_x, col_y, conv_w, conv_b, w_r, b_r, w_i, b_i, lam, tm):
    t = p_a.shape[0]
    c = conv_b.shape[0]
    nt = seq // tm
    rpb = tm // SUBLANES
    bx, by = col_x // c, col_y // c
    assert col_x % c == 0 and col_y % c == 0
    nblk, bw = w_r.shape[0], w_r.shape[1]
    per = max(1, (2 * LANES) // bw)
    nq = nblk // per

    def blockdiag(w):
        w = w.reshape(nq, per, bw, bw)
        eye = jnp.eye(per, dtype=w.dtype)
        return jnp.einsum('qpab,pr->qparb', w, eye).reshape(nq, per * bw, per * bw).astype(BF16)

    vec = lambda z: z.reshape(1, c).astype(F32)
    sp = jax.nn.softplus(-lam)
    return pl.pallas_call(
        functools.partial(_lru_kernel, tm=tm, nq=nq),
        grid=(bsz, nt),
        in_specs=[
            pl.BlockSpec((tm, c), lambda b, i: (b * nt + i, bx)),
            pl.BlockSpec((SUBLANES, c), lambda b, i: (jnp.maximum((b * nt + i) * rpb - 1, 0), bx)),
            pl.BlockSpec((tm, c), lambda b, i: (b * nt + i, by)),
            pl.BlockSpec((CONV_W, c), lambda b, i: (0, 0)),
            pl.BlockSpec((1, c), lambda b, i: (0, 0)),
            pl.BlockSpec((nq, per * bw, per * bw), lambda b, i: (0, 0, 0)),
            pl.BlockSpec((1, c), lambda b, i: (0, 0)),
            pl.BlockSpec((nq, per * bw, per * bw), lambda b, i: (0, 0, 0)),
            pl.BlockSpec((1, c), lambda b, i: (0, 0)),
            pl.BlockSpec((1, c), lambda b, i: (0, 0)),
        ],
        out_specs=pl.BlockSpec((tm, c), lambda b, i: (b * nt + i, 0)),
        out_shape=jax.ShapeDtypeStruct((t, c), BF16),
        scratch_shapes=[pltpu.VMEM((1, c), F32)],
        compiler_params=_cp(("parallel", "arbitrary")),
        name="lru",
    )(p_a, p_a, p_a, conv_w.astype(F32), vec(conv_b), blockdiag(w_r), vec(b_r), blockdiag(w_i), vec(b_i),
      vec(sp))


def _xattn_kernel(q_ref, k_ref, v_ref, o_ref, *, hd):
    scale = hd ** -0.5
    outs = []
    for h in range(XA_HEADS):
        sl = slice(h * hd, (h + 1) * hd)
        s = _mm_nt(q_ref[:, sl], k_ref[:, sl]) * scale
        m = jnp.max(s, axis=-1, keepdims=True)
        e = jnp.exp(s - m)
        pr = e / jnp.sum(e, axis=-1, keepdims=True)
        outs.append(_mm(pr.astype(BF16), v_ref[:, sl]))
    o_ref[...] = jnp.concatenate(outs, axis=1).astype(o_ref.dtype)


def xattn_branch(p_b, kv, bsz, seq, tm):
    t = p_b.shape[0]
    c = kv.shape[1] // 2
    mlen = kv.shape[0] // bsz
    nt = seq // tm
    return pl.pallas_call(
        functools.partial(_xattn_kernel, hd=c // XA_HEADS),
        grid=(bsz, nt),
        in_specs=[
            pl.BlockSpec((tm, c), lambda b, i: (b * nt + i, 0)),
            pl.BlockSpec((mlen, c), lambda b, i: (b, 0)),
            pl.BlockSpec((mlen, c), lambda b, i: (b, 1)),
        ],
        out_specs=pl.BlockSpec((tm, c), lambda b, i: (b * nt + i, 0)),
        out_shape=jax.ShapeDtypeStruct((t, c), BF16),
        compiler_params=_cp(("parallel", "parallel")),
        name="xattn",
    )(p_b, kv, kv)


def _merge_kernel(ya_ref, yb_ref, yc_ref, wa_ref, wb_ref, wc_ref, ga_ref, gb_ref, gc_ref, o_ref):
    acc = _sigmoid(ga_ref[...].astype(F32)) * _mm(ya_ref[...], wa_ref[...])
    acc = acc + _sigmoid(gb_ref[...].astype(F32)) * _mm(yb_ref[...], wb_ref[...])
    acc = acc + _sigmoid(gc_ref[...].astype(F32)) * _mm(yc_ref[...], wc_ref[...])
    o_ref[...] = acc.astype(o_ref.dtype)


def merge(y_a, y_b, y_c, w_a, w_b, w_c, p_b, gate_col0, tm, tn):
    t, c = y_a.shape
    d = w_a.shape[1]
    g0 = gate_col0 // tn
    gstep = d // tn
    assert gate_col0 % tn == 0
    yspec = pl.BlockSpec((tm, c), lambda i, j: (i, 0))
    wspec = pl.BlockSpec((c, tn), lambda i, j: (0, j))
    gspec = lambda n: pl.BlockSpec((tm, tn), lambda i, j: (i, g0 + n * gstep + j))
    return pl.pallas_call(
        _merge_kernel,
        grid=(t // tm, d // tn),
        in_specs=[yspec, yspec, yspec, wspec, wspec, wspec, gspec(0), gspec(1), gspec(2)],
        out_specs=pl.BlockSpec((tm, tn), lambda i, j: (i, j)),
        out_shape=jax.ShapeDtypeStruct((t, d), BF16),
        compiler_params=_cp(("parallel", "parallel")),
        name="merge",
    )(y_a, y_b, y_c, w_a, w_b, w_c, p_b, p_b, p_b)


def _wo_kernel(x_ref, m_ref, wo_ref, g_ref, wrh_ref, wrl_ref, br_ref, h_ref, xn_ref, lg_ref):
    h = x_ref[...].astype(F32) + _mm(m_ref[...], wo_ref[...])
    h_ref[...] = h
    ms = jnp.mean(h * h, axis=-1, keepdims=True)
    xn = h * lax.rsqrt(ms + EPS) * g_ref[...]
    xn_ref[...] = xn
    hi, lo = _split(xn)
    lg_ref[...] = _mm(hi, wrh_ref[...]) + _mm(lo, wrh_ref[...]) + _mm(hi, wrl_ref[...]) + br_ref[...]


def wo_block(x, merged, w_o, g_ffn, w_router, b_router, tm):
    t, d = x.shape
    ne = w_router.shape[1]
    wr = jnp.pad(w_router.astype(F32), ((0, 0), (0, LANES - ne)))
    wr_hi = wr.astype(BF16)
    wr_lo = (wr - wr_hi.astype(F32)).astype(BF16)
    br = jnp.pad(b_router.astype(F32), (0, LANES - ne), constant_values=NEG_BIG).reshape(1, LANES)
    row = pl.BlockSpec((tm, d), lambda i: (i, 0))
    full = lambda r, c: pl.BlockSpec((r, c), lambda i: (0, 0))
    return pl.pallas_call(
        _wo_kernel,
        grid=(t // tm,),
        in_specs=[row, row, full(d, d), full(1, d), full(d, LANES), full(d, LANES), full(1, LANES)],
        out_specs=[row, row, pl.BlockSpec((tm, LANES), lambda i: (i, 0))],
        out_shape=[jax.ShapeDtypeStruct((t, d), F32), jax.ShapeDtypeStruct((t, d), F32),
                   jax.ShapeDtypeStruct((t, LANES), F32)],
        compiler_params=_cp(("parallel",)),
        name="wo",
    )(x, merged, w_o.astype(BF16), g_ffn.reshape(1, d).astype(F32), wr_hi, wr_lo, br)


def _router_kernel(lg_ref, idx_ref, gate_ref, rank_ref, cnt_ref, carry_ref, *, tm):
    i = pl.program_id(0)

    @pl.when(i == 0)
    def _():
        carry_ref[...] = jnp.zeros_like(carry_ref)

    l = lg_ref[...]
    lane = lax.broadcasted_iota(jnp.int32, l.shape, 1).astype(F32)
    vals, sels, idxs = [], [], []
    onehot = jnp.zeros(l.shape, F32)
    for _ in range(TOP_K):
        m = jnp.max(l, axis=-1, keepdims=True)
        idx = jnp.min(jnp.where(l == m, lane, float(LANES)), axis=-1, keepdims=True)
        sel = lane == idx
        vals.append(m)
        sels.append(sel)
        idxs.append(idx)
        onehot = onehot + jnp.where(sel, 1.0, 0.0)
        l = jnp.where(sel, -jnp.inf, l)
    es = [jnp.exp(vv - vals[0]) for vv in vals]
    den = es[0] + es[1] + es[2] + es[3]

    trow = lax.broadcasted_iota(jnp.int32, (tm, tm), 0)
    tcol = lax.broadcasted_iota(jnp.int32, (tm, tm), 1)
    tri = jnp.where(tcol < trow, 1.0, 0.0).astype(BF16)
    cum = _mm(tri, onehot.astype(BF16)) + carry_ref[...]
    carry = carry_ref[...] + jnp.sum(onehot, axis=0, keepdims=True)
    carry_ref[...] = carry
    cnt_ref[...] = carry

    idx_out = jnp.zeros(l.shape, F32)
    gate_out = jnp.zeros(l.shape, F32)
    rank_out = jnp.zeros(l.shape, F32)
    for kk in range(TOP_K):
        rk = jnp.sum(jnp.where(sels[kk], cum, 0.0), axis=-1, keepdims=True)
        idx_out = jnp.where(lane == kk, idxs[kk], idx_out)
        gate_out = jnp.where(lane == kk, es[kk] / den, gate_out)
        rank_out = jnp.where(lane == kk, rk, rank_out)
    idx_ref[...] = idx_out.astype(jnp.int32)
    gate_ref[...] = gate_out
    rank_ref[...] = rank_out.astype(jnp.int32)


def router(logits, tm):
    t = logits.shape[0]
    spec = pl.BlockSpec((tm, LANES), lambda i: (i, 0))
    return pl.pallas_call(
        functools.partial(_router_kernel, tm=tm),
        grid=(t // tm,),
        in_specs=[spec],
        out_specs=[spec, spec, spec, pl.BlockSpec((1, LANES), lambda i: (0, 0))],
        out_shape=[jax.ShapeDtypeStruct((t, LANES), jnp.int32), jax.ShapeDtypeStruct((t, LANES), F32),
                   jax.ShapeDtypeStruct((t, LANES), jnp.int32), jax.ShapeDtypeStruct((1, LANES), F32)],
        scratch_shapes=[pltpu.VMEM((1, LANES), F32)],
        compiler_params=_cp(("arbitrary",)),
        name="router",
    )(logits)


def _dispatch_kernel(dest_ref, zrow_ref, meta_ref, x_ref, rows_ref, zero_ref, zsem, sem, *, tm, ne, n_sub_total):
    i = pl.program_id(0)

    @pl.when(i == 0)
    def _():
        zero_ref[...] = jnp.zeros_like(zero_ref)

        def zcopy(row0):
            return pltpu.make_async_copy(
                zero_ref, rows_ref.at[pl.ds(pl.multiple_of(row0, SUB_ROWS), SUB_ROWS), :], zsem)

        for e in range(ne):
            zcopy(zrow_ref[e]).start()

        def tail(sb, carry):
            zcopy(sb * SUB_ROWS).start()
            return carry

        lax.fori_loop(meta_ref[0], n_sub_total, tail, 0)
        for e in range(ne):
            zcopy(zrow_ref[e]).wait()

        def tail_wait(sb, carry):
            zcopy(sb * SUB_ROWS).wait()
            return carry

        lax.fori_loop(meta_ref[0], n_sub_total, tail_wait, 0)

    base = i * (tm * TOP_K)

    def row_copy(r, dst_row):
        return pltpu.make_async_copy(x_ref.at[pl.ds(r, 1), :], rows_ref.at[pl.ds(dst_row, 1), :], sem)

    def issue(r, carry):
        for kk in range(TOP_K):
            row_copy(r, dest_ref[base + r * TOP_K + kk]).start()
        return carry

    lax.fori_loop(0, tm, issue, 0, unroll=8)
    for kk in range(TOP_K):
        pltpu.make_async_copy(x_ref, rows_ref.at[pl.ds(0, tm), :], sem).wait()


def dispatch(xn, dest_flat, zrow, used_sub, n_rows, tm):
    t, d = xn.shape
    ne = zrow.shape[0]
    grid_spec = pltpu.PrefetchScalarGridSpec(
        num_scalar_prefetch=3,
        grid=(t // tm,),
        in_specs=[pl.BlockSpec((tm, d), lambda i, *_: (i, 0))],
        out_specs=pl.BlockSpec(memory_space=pl.ANY),
        scratch_shapes=[pltpu.VMEM((SUB_ROWS, d), xn.dtype), pltpu.SemaphoreType.DMA, pltpu.SemaphoreType.DMA],
    )
    return pl.pallas_call(
        functools.partial(_dispatch_kernel, tm=tm, ne=ne, n_sub_total=n_rows // SUB_ROWS),
        grid_spec=grid_spec,
        out_shape=jax.ShapeDtypeStruct((n_rows, d), xn.dtype),
        compiler_params=_cp(("arbitrary",)),
        name="dispatch",
    )(dest_flat, zrow, used_sub, xn)


def _expert_kernel(ge_ref, r0_ref, ns_ref, meta_ref, rows_ref, wg_ref, bg_ref, wu_ref, bu_ref, wd_ref, bd_ref,
                   out_ref, stage_ref, xb_ref, acc_ref, ost_ref, xsem, osem, *, n_sub_total):
    g = pl.program_id(0)
    j = pl.program_id(1)
    nj = pl.num_programs(1)
    ng = meta_ref[1]
    used = g < ng
    nsub = ns_ref[g]
    d = stage_ref.shape[1]

    def x_copy(grp, sidx):
        row0 = pl.multiple_of(r0_ref[grp] + sidx * SUB_ROWS, SUB_ROWS)
        return pltpu.make_async_copy(rows_ref.at[pl.ds(row0, SUB_ROWS), :],
                                     stage_ref.at[pl.ds(sidx * SUB_ROWS, SUB_ROWS), :], xsem)

    def o_copy(row0, sidx):
        return pltpu.make_async_copy(ost_ref.at[pl.ds(sidx * SUB_ROWS, SUB_ROWS), :],
                                     out_ref.at[pl.ds(pl.multiple_of(row0, SUB_ROWS), SUB_ROWS), :], osem)

    def for_subs(count, fn):
        for sidx in range(GROUP_SUBS):
            @pl.when(sidx < count)
            def _():
                fn(sidx)

    @pl.when((g == 0) & (j == 0))
    def _():
        for_subs(ns_ref[0], lambda sidx: x_copy(0, sidx).start())
        ost_ref[0:SUB_ROWS, :] = jnp.zeros((SUB_ROWS, d), F32)

        def tail(sb, carry):
            cp = o_copy(sb * SUB_ROWS, 0)
            cp.start()
            cp.wait()
            return carry

        lax.fori_loop(meta_ref[0], n_sub_total, tail, 0)

    @pl.when(used & (j == 0))
    def _():
        for_subs(nsub, lambda sidx: x_copy(g, sidx).wait())

        def stage_in(sidx):
            sl = pl.ds(sidx * SUB_ROWS, SUB_ROWS)
            xb_ref[sl, :] = stage_ref[sl, :].astype(BF16)
            acc_ref[sl, :] = jnp.zeros((SUB_ROWS, d), F32)

        for_subs(nsub, stage_in)

        @pl.when(g + 1 < ng)
        def _():
            for_subs(ns_ref[g + 1], lambda sidx: x_copy(g + 1, sidx).start())

    @pl.when(used)
    def _():
        for n in range(1, GROUP_SUBS + 1):
            @pl.when(nsub == n)
            def _():
                rows = n * SUB_ROWS
                xb = xb_ref[0:rows, :]
                hg = jnp.minimum(_mm(xb, wg_ref[...].astype(BF16)) + bg_ref[...], SWIGLU_LIMIT)
                hu = jnp.clip(_mm(xb, wu_ref[...].astype(BF16)) + bu_ref[...], -SWIGLU_LIMIT, SWIGLU_LIMIT)
                hh = (hu + 1.0) * hg * _sigmoid(SWIGLU_ALPHA * hg)
                acc_ref[0:rows, :] += _mm(hh.astype(BF16), wd_ref[...].astype(BF16))

    @pl.when(used & (j == nj - 1))
    def _():
        @pl.when(g > 0)
        def _():
            for_subs(ns_ref[g - 1], lambda sidx: o_copy(r0_ref[g - 1] + sidx * SUB_ROWS, sidx).wait())

        def result_out(sidx):
            sl = pl.ds(sidx * SUB_ROWS, SUB_ROWS)
            ost_ref[sl, :] = acc_ref[sl, :] + bd_ref[...]
            o_copy(r0_ref[g] + sidx * SUB_ROWS, sidx).start()

        for_subs(nsub, result_out)

        @pl.when(g == ng - 1)
        def _():
            for_subs(nsub, lambda sidx: o_copy(r0_ref[g] + sidx * SUB_ROWS, sidx).wait())


def experts(rows, g_expert, g_row0, g_nsub, meta, w_gate, b_gate, w_up, b_up, w_down, b_down, tn):
    n_rows, d = rows.shape
    ne, _, de = w_gate.shape
    n_groups_max = g_expert.shape[0]
    nj = de // tn
    gr = GROUP_SUBS * SUB_ROWS

    def eb(g, ge, meta):
        return ge[jnp.minimum(g, meta[1] - 1)]

    def jb(g, j, meta):
        return jnp.where(g < meta[1], j, nj - 1)

    grid_spec = pltpu.PrefetchScalarGridSpec(
        num_scalar_prefetch=4,
        grid=(n_groups_max, nj),
        in_specs=[
            pl.BlockSpec(memory_space=pl.ANY),
            pl.BlockSpec((None, d, tn), lambda g, j, ge, r0, ns, mt: (eb(g, ge, mt), 0, jb(g, j, mt))),
            pl.BlockSpec((None, 1, tn), lambda g, j, ge, r0, ns, mt: (eb(g, ge, mt), 0, jb(g, j, mt))),
            pl.BlockSpec((None, d, tn), lambda g, j, ge, r0, ns, mt: (eb(g, ge, mt), 0, jb(g, j, mt))),
            pl.BlockSpec((None, 1, tn), lambda g, j, ge, r0, ns, mt: (eb(g, ge, mt), 0, jb(g, j, mt))),
            pl.BlockSpec((None, tn, d), lambda g, j, ge, r0, ns, mt: (eb(g, ge, mt), jb(g, j, mt), 0)),
            pl.BlockSpec((None, 1, d), lambda g, j, ge, r0, ns, mt: (eb(g, ge, mt), 0, 0)),
        ],
        out_specs=pl.BlockSpec(memory_space=pl.ANY),
        scratch_shapes=[
            pltpu.VMEM((gr, d), F32),
            pltpu.VMEM((gr, d), BF16),
            pltpu.VMEM((gr, d), F32),
            pltpu.VMEM((gr, d), F32),
            pltpu.SemaphoreType.DMA,
            pltpu.SemaphoreType.DMA,
        ],
    )
    return pl.pallas_call(
        functools.partial(_expert_kernel, n_sub_total=n_rows // SUB_ROWS),
        grid_spec=grid_spec,
        out_shape=jax.ShapeDtypeStruct((n_rows, d), F32),
        compiler_params=_cp(("arbitrary", "arbitrary")),
        name="experts",
    )(g_expert, g_row0, g_nsub, meta, rows, w_gate, b_gate.reshape(ne, 1, de), w_up, b_up.reshape(ne, 1, de),
      w_down, b_down.reshape(ne, 1, d))


def _combine_kernel(dest_ref, rows_ref, h_ref, gate_ref, g_ref, o_ref, buf_ref, sem, *, tm):
    i = pl.program_id(0)
    nt = pl.num_programs(0)
    n = tm * TOP_K
    slot = i % 2

    def issue_tile(tile, slt):
        base = tile * n

        def issue(r, carry):
            for kk in range(TOP_K):
                pltpu.make_async_copy(rows_ref.at[pl.ds(dest_ref[base + r * TOP_K + kk], 1), :],
                                      buf_ref.at[slt, pl.ds(kk * tm + r, 1), :], sem.at[slt]).start()
            return carry

        lax.fori_loop(0, tm, issue, 0, unroll=8)

    @pl.when(i == 0)
    def _():
        issue_tile(0, 0)

    pltpu.make_async_copy(rows_ref.at[pl.ds(0, n), :], buf_ref.at[slot], sem.at[slot]).wait()

    @pl.when(i + 1 < nt)
    def _():
        issue_tile(i + 1, 1 - slot)

    gates = gate_ref[...]
    h = h_ref[...]
    for kk in range(TOP_K):
        h = h + gates[:, kk:kk + 1] * buf_ref[slot, kk * tm:(kk + 1) * tm, :]
    ms = jnp.mean(h * h, axis=-1, keepdims=True)
    o_ref[...] = (h * lax.rsqrt(ms + EPS) * g_ref[...]).astype(o_ref.dtype)


def combine(dest_flat, out_rows, h1, gates, g_final, tm, out_dtype):
    t, d = h1.shape
    grid_spec = pltpu.PrefetchScalarGridSpec(
        num_scalar_prefetch=1,
        grid=(t // tm,),
        in_specs=[
            pl.BlockSpec(memory_space=pl.ANY),
            pl.BlockSpec((tm, d), lambda i, *_: (i, 0)),
            pl.BlockSpec((tm, LANES), lambda i, *_: (i, 0)),
            pl.BlockSpec((1, d), lambda i, *_: (0, 0)),
        ],
        out_specs=pl.BlockSpec((tm, d), lambda i, *_: (i, 0)),
        scratch_shapes=[pltpu.VMEM((2, tm * TOP_K, d), out_rows.dtype), pltpu.SemaphoreType.DMA((2,))],
    )
    return pl.pallas_call(
        functools.partial(_combine_kernel, tm=tm),
        grid_spec=grid_spec,
        out_shape=jax.ShapeDtypeStruct((t, d), out_dtype),
        compiler_params=_cp(("arbitrary",)),
        name="combine",
    )(dest_flat, out_rows, h1, gates, g_final.reshape(1, d).astype(F32))


def _tile(n, pref):
    while n % pref:
        pref //= 2
    return pref


def _group_tables(counts, n_groups_max):
    ne = counts.shape[0]
    nsub_e = (counts + SUB_ROWS - 1) // SUB_ROWS
    pend_sub = jnp.cumsum(nsub_e)
    pstart = (pend_sub - nsub_e) * SUB_ROWS
    ng_e = (nsub_e + GROUP_SUBS - 1) // GROUP_SUBS
    base_e = nsub_e // jnp.maximum(ng_e, 1)
    rem_e = nsub_e - base_e * ng_e
    cum_ng = jnp.cumsum(ng_e)
    n_groups = cum_ng[-1]
    g = jnp.arange(n_groups_max, dtype=jnp.int32)
    e_g = jnp.minimum(jnp.sum((g[:, None] >= cum_ng[None, :]).astype(jnp.int32), axis=1), ne - 1)
    lg = g - (cum_ng[e_g] - ng_e[e_g])
    valid = g < n_groups
    g_nsub = jnp.where(valid, base_e[e_g] + (lg < rem_e[e_g]).astype(jnp.int32), 0)
    g_row0 = jnp.where(valid, pstart[e_g] + SUB_ROWS * (lg * base_e[e_g] + jnp.minimum(lg, rem_e[e_g])), 0)
    zrow = pstart + jnp.maximum(nsub_e - 1, 0) * SUB_ROWS
    meta = jnp.stack([pend_sub[-1], n_groups]).astype(jnp.int32)
    i32 = lambda z: z.astype(jnp.int32)
    return i32(pstart), i32(zrow), i32(e_g), i32(g_row0), i32(g_nsub), meta


def kernel(x, mem, g_mix, w_in, mu_rwkv, rwkv_w0, rwkv_w2, rwkv_a0, rwkv_a2, rwkv_g2, rwkv_k_k, rwkv_k_a,
           rwkv_r_k, rwkv_ln_w, rwkv_ln_b, w_rwkv_out, lru_conv_w, lru_conv_b, lru_w_r, lru_b_r, lru_w_i,
           lru_b_i, lru_lambda, w_lru_out, g_mem, w_mem_k, w_mem_v, w_xa_out, w_o, g_ffn, w_router,
           b_router, w_gate, b_gate, w_up, b_up, w_down, b_down, g_final):
    bsz, seq, d = x.shape
    assert w_in.shape[0] == 1, "single-layer block"
    l = 0
    t = bsz * seq
    c = rwkv_w0.shape[1]
    n_lora = LORA_W + LORA_A + LORA_G
    ne = w_router.shape[2]
    assert lru_conv_b.shape[1] == c and w_mem_k.shape[2] == c and c % LORA_PAD == 0

    h = x.reshape(t, d).astype(F32)
    memf = mem.reshape(bsz * mem.shape[1], d)
    wi = w_in[l]
    c0 = 3 * c + n_lora
    w_a = jnp.concatenate(
        [wi[:, 0:3 * c], wi[:, c0:c0 + 2 * c], wi[:, 3 * c:c0], jnp.zeros((d, LORA_PAD - n_lora), wi.dtype)],
        axis=1).astype(BF16)
    w_b = wi[:, c0 + 2 * c:].astype(BF16)

    xn = norm_cast(h, g_mix[l], _tile(t, 512))
    p_a = matmul(xn, w_a, _tile(t, 1024), _tile(w_a.shape[1], 512), F32)
    p_b = matmul(xn, w_b, _tile(t, 1024), _tile(w_b.shape[1], 512), BF16)
    mn = norm_cast(memf, g_mem[l], _tile(memf.shape[0], 256))
    kv = matmul(mn, jnp.concatenate([w_mem_k[l], w_mem_v[l]], axis=1).astype(BF16),
                _tile(memf.shape[0], 256), _tile(2 * c, 512), BF16)

    y_a = rwkv_branch(p_a, bsz, seq, mu_rwkv[l], rwkv_w0[l], rwkv_w2[l], rwkv_a0[l], rwkv_a2[l],
                      rwkv_g2[l], rwkv_k_k[l], rwkv_k_a[l], rwkv_r_k[l].reshape(-1), rwkv_ln_w[l],
                      rwkv_ln_b[l], _tile(seq, 256), 5 * c)
    y_b = lru_branch(p_a, bsz, seq, 3 * c, 4 * c, lru_conv_w[l], lru_conv_b[l], lru_w_r[l], lru_b_r[l],
                     lru_w_i[l], lru_b_i[l], lru_lambda[l], _tile(seq, 256))
    y_c = xattn_branch(p_b, kv, bsz, seq, _tile(seq, 512))
    merged = merge(y_a, y_b, y_c, w_rwkv_out[l].astype(BF16), w_lru_out[l].astype(BF16),
                   w_xa_out[l].astype(BF16), p_b, c, _tile(t, 512), _tile(d, 512))
    h1, xn2, logits = wo_block(h, merged, w_o[l], g_ffn[l], w_router[l], b_router[l], _tile(t, 256))

    idx, gates, rank, counts = router(logits, _tile(t, 512))
    counts = counts[0, :ne].astype(jnp.int32)
    n_rows = t * TOP_K + ne * SUB_ROWS
    n_groups_max = (t * TOP_K) // (GROUP_SUBS * SUB_ROWS) + ne
    pstart, zrow, g_expert, g_row0, g_nsub, meta = _group_tables(counts, n_groups_max)
    dest = (pstart[idx[:, :TOP_K]] + rank[:, :TOP_K]).reshape(-1)

    tmd = _tile(t, 256)
    rows = dispatch(xn2, dest, zrow, meta, n_rows, tmd)
    out_rows = experts(rows, g_expert, g_row0, g_nsub, meta, w_gate[l], b_gate[l], w_up[l], b_up[l],
                       w_down[l], b_down[l], _tile(w_gate.shape[3], 256))
    out = combine(dest, out_rows, h1, gates, g_final, tmd, x.dtype)
    return out.reshape(bsz, seq, d)
```

```python
import functools

import jax
import jax.numpy as jnp
from jax import lax
from jax.experimental import pallas as pl
from jax.experimental.pallas import tpu as pltpu

F32 = jnp.float32
BF16 = jnp.bfloat16

EPS = 1e-6
GN_EPS = 64e-5
RG_C = 8.0
SWIGLU_LIMIT = 7.0
SWIGLU_ALPHA = 1.702
RWKV_HEAD = 64
LORA_W = 64
LORA_A = 64
LORA_G = 160
CONV_W = 4
LRU_BLOCKS = 16
XA_HEADS = 4
TOP_K = 4
N_BRANCH = 3

LANES = 128
SUBLANES = 8
CHUNK = 64
SUB = 16
SUB_ROWS = 256
GROUP_SUBS = 4
LORA_PAD = 512
NEG_BIG = -1e30
VMEM_LIMIT = 56 * 1024 * 1024


def _cp(sem, vmem=None):
    return pltpu.CompilerParams(dimension_semantics=sem, vmem_limit_bytes=vmem or VMEM_LIMIT)


def _mm(a, b):
    return jnp.dot(a, b, preferred_element_type=F32)


def _mm_nt(a, b):
    return lax.dot_general(a, b, (((1,), (1,)), ((), ())), preferred_element_type=F32)


def _mm_tn(a, b):
    return lax.dot_general(a, b, (((0,), (0,)), ((), ())), preferred_element_type=F32)


def _split(x):
    hi = x.astype(BF16)
    lo = (x - hi.astype(F32)).astype(BF16)
    return hi, lo


def _mm_exact_rhs(x, m):
    hi, lo = _split(x)
    return _mm(hi, m) + _mm(lo, m)


def _mm_exact_lhs(m, x):
    hi, lo = _split(x)
    return _mm(m, hi) + _mm(m, lo)


def _sigmoid(x):
    return 1.0 / (1.0 + jnp.exp(-x))


def _softplus(x):
    return jnp.maximum(x, 0.0) + jnp.log(1.0 + jnp.exp(-jnp.abs(x)))


def _norm_kernel(x_ref, g_ref, o_ref):
    x = x_ref[...].astype(F32)
    ms = jnp.mean(x * x, axis=-1, keepdims=True)
    o_ref[...] = (x * lax.rsqrt(ms + EPS) * g_ref[...]).astype(o_ref.dtype)


def norm_cast(x, g, tm, out_dtype=BF16):
    t, d = x.shape
    return pl.pallas_call(
        _norm_kernel,
        grid=(t // tm,),
        in_specs=[pl.BlockSpec((tm, d), lambda i: (i, 0)), pl.BlockSpec((1, d), lambda i: (0, 0))],
        out_specs=pl.BlockSpec((tm, d), lambda i: (i, 0)),
        out_shape=jax.ShapeDtypeStruct((t, d), out_dtype),
        compiler_params=_cp(("parallel",)),
        name="norm_cast",
    )(x, g.reshape(1, d).astype(F32))


def _matmul_kernel(a_ref, w_ref, o_ref):
    o_ref[...] = _mm(a_ref[...], w_ref[...]).astype(o_ref.dtype)


def matmul(a, w, tm, tn, out_dtype):
    m, k = a.shape
    n = w.shape[1]
    return pl.pallas_call(
        _matmul_kernel,
        grid=(m // tm, n // tn),
        in_specs=[pl.BlockSpec((tm, k), lambda i, j: (i, 0)), pl.BlockSpec((k, tn), lambda i, j: (0, j))],
        out_specs=pl.BlockSpec((tm, tn), lambda i, j: (i, j)),
        out_shape=jax.ShapeDtypeStruct((m, n), out_dtype),
        compiler_params=_cp(("parallel", "parallel")),
        name="matmul",
    )(a, w)


def _rwkv_kernel(r_ref, k_ref, v_ref, lo_ref, rp_ref, kp_ref, vp_ref, lop_ref,
                 mur_ref, muk_ref, muv_ref, mulo_ref, w0_ref, a0_ref, kk_ref, ka_ref, rk_ref,
                 lnw_ref, lnb_ref, w2_ref, a2_ref, g2_ref, o_ref, s_ref, *, tm, nb):
    i = pl.program_id(1)
    first = i == 0

    @pl.when(first)
    def _():
        s_ref[...] = jnp.zeros_like(s_ref)

    bf = lambda z: z.astype(BF16)
    bs = range(nb)

    def mix(x_ref, p_ref, mu_ref, b):
        x = x_ref[b]
        prev = jnp.where(first, 0.0, p_ref[b, SUBLANES - 1:SUBLANES, :])
        row = lax.broadcasted_iota(jnp.int32, x.shape, 0)
        sh = jnp.where(row == 0, prev, pltpu.roll(x, 1, axis=0))
        return x + (sh - x) * mu_ref[...]

    r = [mix(r_ref, rp_ref, mur_ref, b) for b in bs]
    k = [mix(k_ref, kp_ref, muk_ref, b) for b in bs]
    v = [mix(v_ref, vp_ref, muv_ref, b) for b in bs]
    lo = [mix(lo_ref, lop_ref, mulo_ref, b) for b in bs]

    wl = [_mm(bf(jnp.tanh(z[:, 0:LANES])), w2_ref[...]) for z in lo]
    al = [_mm(bf(z[:, 0:LANES]), a2_ref[...]) for z in lo]
    g = [_mm(bf(_sigmoid(z[:, LANES:3 * LANES])), g2_ref[...]) for z in lo]

    ld = [-jnp.exp(-_softplus(-(w0_ref[...] + z)) - 0.5) for z in wl]
    a_sig = [_sigmoid(a0_ref[...] + z) for z in al]

    lane = lax.broadcasted_iota(jnp.int32, (LANES, LANES), 1)
    sub = lax.broadcasted_iota(jnp.int32, (LANES, LANES), 0)
    e2 = jnp.where((lane // RWKV_HEAD) == (sub // RWKV_HEAD), 1.0, 0.0).astype(BF16)

    kk = [z * kk_ref[...] for z in k]
    kk = [z * lax.rsqrt(jnp.maximum(_mm_exact_rhs(z * z, e2), 1e-24)) for z in kk]
    k2 = [k[b] * (1.0 + (a_sig[b] - 1.0) * ka_ref[...]) for b in bs]
    a_vec = [-z for z in kk]
    b_vec = [kk[b] * a_sig[b] for b in bs]

    trow = lax.broadcasted_iota(jnp.int32, (tm, tm), 0)
    tcol = lax.broadcasted_iota(jnp.int32, (tm, tm), 1)
    tri = jnp.where(((trow // CHUNK) == (tcol // CHUNK)) & (tcol <= trow), 1.0, 0.0).astype(BF16)
    c = [_mm_exact_lhs(tri, z) for z in ld]

    lane_row = lax.broadcasted_iota(jnp.int32, (CHUNK, LANES), 1)
    head_a = lane_row < RWKV_HEAD

    def stack(x):
        return jnp.concatenate([jnp.where(head_a, x, 0.0), jnp.where(head_a, 0.0, x)], axis=0)

    n2 = 2 * CHUNK
    ri = lax.broadcasted_iota(jnp.int32, (n2, n2), 0)
    ci = lax.broadcasted_iota(jnp.int32, (n2, n2), 1)
    strict = ri > ci
    incl = ri >= ci
    blk = (ri // SUB) == (ci // SUB)
    eye = jnp.where(ri == ci, 1.0, 0.0)

    nch = tm // CHUNK
    items = [(b, q) for q in range(nch) for b in bs]
    rng = range(len(items))
    sl = lambda q: slice(q * CHUNK, (q + 1) * CHUNK)
    cs = [c[b][sl(q)] for b, q in items]
    lds = [ld[b][sl(q)] for b, q in items]
    c_ends = [z[CHUNK - 1:CHUNK, :] for z in cs]
    e_pos = [jnp.exp(z) for z in cs]
    e_neg = [jnp.exp(-z) for z in cs]
    e_end = [jnp.exp(ce - z) for ce, z in zip(c_ends, cs)]
    p_end = [jnp.exp(ce) for ce in c_ends]
    rt32 = [stack(r[b][sl(q)] * e_pos[n]) for n, (b, q) in enumerate(items)]
    at = [bf(stack(a_vec[b][sl(q)] * jnp.exp(cs[n] - lds[n]))) for n, (b, q) in enumerate(items)]
    rt = [bf(z) for z in rt32]
    bt = [bf(stack(b_vec[b][sl(q)] * e_neg[n])) for n, (b, q) in enumerate(items)]
    kt = [bf(stack(k2[b][sl(q)] * e_neg[n])) for n, (b, q) in enumerate(items)]
    bh = [bf(stack(b_vec[b][sl(q)] * e_end[n])) for n, (b, q) in enumerate(items)]
    kh = [bf(stack(k2[b][sl(q)] * e_end[n])) for n, (b, q) in enumerate(items)]
    vs = [bf(stack(v[b][sl(q)])) for b, q in items]

    gram = [_mm_nt(jnp.concatenate([at[n], rt[n]], axis=0), jnp.concatenate([bt[n], kt[n]], axis=0))
            for n in rng]
    l_ab = [jnp.where(strict, z[0:n2, 0:n2], 0.0) for z in gram]
    l_ak = [bf(jnp.where(strict, z[0:n2, n2:], 0.0)) for z in gram]
    m_rb = [bf(jnp.where(incl, z[n2:, 0:n2], 0.0)) for z in gram]
    m_rk = [bf(jnp.where(incl, z[n2:, n2:], 0.0)) for z in gram]

    d = [jnp.where(blk, z, 0.0) for z in l_ab]
    l_off = [bf(l_ab[n] - d[n]) for n in rng]
    db = [bf(z) for z in d]
    p = [eye + z for z in d]
    d2 = [bf(_mm(z, z)) for z in db]
    lv = [_mm(l_ak[n], vs[n]) for n in rng]
    p = [p[n] + _mm(d2[n], bf(p[n])) for n in rng]
    d4 = [bf(_mm(z, z)) for z in d2]
    p = [p[n] + _mm(d4[n], bf(p[n])) for n in rng]
    d8 = [bf(_mm(z, z)) for z in d4]
    td = [bf(p[n] + _mm(d8[n], bf(p[n]))) for n in rng]
    mm = [bf(_mm(td[n], l_off[n])) for n in rng]
    x = [_mm(td[n], jnp.concatenate([at[n], bf(lv[n])], axis=1)) for n in rng]
    m2 = [bf(_mm(z, z)) for z in mm]
    x = [x[n] + _mm(mm[n], bf(x[n])) for n in rng]
    xb = [bf(x[n] + _mm(m2[n], bf(x[n]))) for n in rng]
    wb = [z[:, 0:n2] for z in xb]
    u0b = [z[:, n2:] for z in xb]

    t1 = [_mm(m_rb[n], xb[n]) for n in rng]
    rp = [bf(rt32[n] + t1[n][:, 0:n2]) for n in rng]
    y0 = [t1[n][:, n2:] + _mm(m_rk[n], vs[n]) for n in rng]
    ab = [bf(_mm_tn(bh[n], wb[n])) for n in rng]
    gt = [_mm_tn(u0b[n], bh[n]) + _mm_tn(vs[n], kh[n]) for n in rng]

    s = [s_ref[b] for b in bs]
    ys = [[] for _ in bs]
    for n, (b, q) in enumerate(items):
        sb = bf(s[b])
        yst = _mm_nt(rp[n], sb) + y0[n]
        s[b] = s[b] * p_end[n] + _mm_nt(sb, ab[n]) + gt[n]
        ys[b].append(yst[0:CHUNK] + yst[CHUNK:n2])
    inv_n = 1.0 / RWKV_HEAD
    for b in bs:
        s_ref[b] = s[b]
    y = [jnp.concatenate(z, axis=0) if len(z) > 1 else z[0] for z in ys]
    mean = [_mm_exact_rhs(z, e2) * inv_n for z in y]
    yc = [y[b] - mean[b] for b in bs]
    var = [_mm_exact_rhs(z * z, e2) * inv_n for z in yc]
    bonus = [_mm_exact_rhs(r[b] * k2[b] * rk_ref[...], e2) * v[b] for b in bs]
    for b in bs:
        yn = yc[b] * lax.rsqrt(var[b] + GN_EPS) * lnw_ref[...] + lnb_ref[...]
        o_ref[b] = ((yn + bonus[b]) * g[b]).astype(o_ref.dtype)


def rwkv_branch(p_a, bsz, seq, mu, w0, w2, a0, a2, g2, k_k, k_a, r_k, ln_w, ln_b, tm, col0_lora):
    t, na = p_a.shape
    c = w0.shape[0]
    nj = c // LANES
    nt = seq // tm
    lw = LORA_PAD
    lora_blk = col0_lora // lw
    assert col0_lora % lw == 0 and seq % tm == 0 and tm % CHUNK == 0
    p3 = p_a.reshape(bsz, seq, na)

    mu_rkv = mu[:3 * c].reshape(1, 3 * c)
    mu_lo = jnp.pad(mu[3 * c:], (0, lw - (LORA_W + LORA_A + LORA_G))).reshape(1, lw)
    w2p = jnp.pad(w2, ((0, LANES - LORA_W), (0, 0))).astype(BF16)
    a2p = jnp.pad(a2, ((LORA_W, LANES - LORA_W - LORA_A), (0, 0))).astype(BF16)
    g2p = jnp.pad(g2, ((0, 2 * LANES - LORA_G), (0, 0))).astype(BF16)
    vec = lambda z: z.reshape(1, c).astype(F32)

    rpb = tm // SUBLANES

    def cur(off):
        return pl.BlockSpec((bsz, tm, LANES), lambda j, i: (0, i, off * nj + j))

    def prev(off):
        return pl.BlockSpec((bsz, SUBLANES, LANES), lambda j, i: (0, jnp.maximum(i * rpb - 1, 0), off * nj + j))

    def slab(rows):
        return pl.BlockSpec((rows, LANES), lambda j, i: (0, j))

    in_specs = [
        cur(0), cur(1), cur(2),
        pl.BlockSpec((bsz, tm, lw), lambda j, i: (0, i, lora_blk)),
        prev(0), prev(1), prev(2),
        pl.BlockSpec((bsz, SUBLANES, lw), lambda j, i: (0, jnp.maximum(i * rpb - 1, 0), lora_blk)),
        pl.BlockSpec((1, LANES), lambda j, i: (0, j)),
        pl.BlockSpec((1, LANES), lambda j, i: (0, nj + j)),
        pl.BlockSpec((1, LANES), lambda j, i: (0, 2 * nj + j)),
        pl.BlockSpec((1, lw), lambda j, i: (0, 0)),
        slab(1), slab(1), slab(1), slab(1), slab(1), slab(1), slab(1),
        slab(LANES), slab(LANES), slab(2 * LANES),
    ]
    out = pl.pallas_call(
        functools.partial(_rwkv_kernel, tm=tm, nb=bsz),
        grid=(nj, nt),
        in_specs=in_specs,
        out_specs=pl.BlockSpec((bsz, tm, LANES), lambda j, i: (0, i, j)),
        out_shape=jax.ShapeDtypeStruct((bsz, seq, c), BF16),
        scratch_shapes=[pltpu.VMEM((bsz, 2 * CHUNK, 2 * CHUNK), F32)],
        compiler_params=_cp(("parallel", "arbitrary")),
        name="rwkv",
    )(p3, p3, p3, p3, p3, p3, p3, p3,
      mu_rkv, mu_rkv, mu_rkv, mu_lo, vec(w0), vec(a0), vec(k_k), vec(k_a), vec(r_k), vec(ln_w), vec(ln_b),
      w2p, a2p, g2p)
    return out.reshape(t, c)


def _lru_kernel(x_ref, xp_ref, y_ref, cw_ref, cb_ref, wr_ref, br_ref, wi_ref, bi_ref, sp_ref,
                o_ref, h_ref, *, tm, nq):
    i = pl.program_id(1)
    first = i == 0

    @pl.when(first)
    def _():
        h_ref[...] = jnp.zeros_like(h_ref)

    x = x_ref[...]
    prev = jnp.where(first, 0.0, xp_ref[...])
    xx = jnp.concatenate([prev, x], axis=0)
    xc = cb_ref[...] + x * cw_ref[0:1, :]
    for j in range(1, CONV_W):
        xc = xc + xx[SUBLANES - j:SUBLANES - j + tm, :] * cw_ref[j:j + 1, :]

    xcb = xc.astype(BF16)
    qw = x.shape[1] // nq
    gr = jnp.concatenate([_mm(xcb[:, q * qw:(q + 1) * qw], wr_ref[q]) for q in range(nq)], axis=1)
    gi = jnp.concatenate([_mm(xcb[:, q * qw:(q + 1) * qw], wi_ref[q]) for q in range(nq)], axis=1)
    gate_r = _sigmoid(gr + br_ref[...])
    gate_i = _sigmoid(gi + bi_ref[...])
    log_a = -RG_C * gate_r * sp_ref[...]
    a = jnp.exp(log_a)
    mult = jnp.sqrt(jnp.maximum(1.0 - jnp.exp(2.0 * log_a), 0.0))
    row = lax.broadcasted_iota(jnp.int32, x.shape, 0)
    mult = jnp.where(first & (row == 0), 1.0, mult)
    b = mult * gate_i * xc

    rin = row % SUBLANES
    sft = 1
    while sft < SUBLANES:
        keep = rin >= sft
        a_sh = jnp.where(keep, pltpu.roll(a, sft, axis=0), 1.0)
        b_sh = jnp.where(keep, pltpu.roll(b, sft, axis=0), 0.0)
        b = b + a * b_sh
        a = a * a_sh
        sft *= 2
    carry = h_ref[...]
    hs = []
    for gi in range(tm // SUBLANES):
        rows = slice(gi * SUBLANES, (gi + 1) * SUBLANES)
        hg = a[rows] * carry + b[rows]
        carry = hg[SUBLANES - 1:SUBLANES, :]
        hs.append(hg)
    h = jnp.concatenate(hs, axis=0)
    h_ref[...] = carry

    py = y_ref[...]
    gelu = 0.5 * py * (1.0 + jnp.tanh(0.7978845608028654 * (py + 0.044715 * py * py * py)))
    o_ref[...] = (h * gelu).astype(o_ref.dtype)


def lru_branch(p_a, bsz, seq, col_x, col_y, conv_w, conv_b, w_r, b_r, w_i, b_i, lam, tm):
    t = p_a.shape[0]
    c = conv_b.shape[0]
    nt = seq // tm
    rpb = tm // SUBLANES
    bx, by = col_x // c, col_y // c
    assert col_x % c == 0 and col_y % c == 0
    nblk, bw = w_r.shape[0], w_r.shape[1]
    per = max(1, (2 * LANES) // bw)
    nq = nblk // per

    def blockdiag(w):
        w = w.reshape(nq, per, bw, bw)
        eye = jnp.eye(per, dtype=w.dtype)
        return jnp.einsum('qpab,pr->qparb', w, eye).reshape(nq, per * bw, per * bw).astype(BF16)

    vec = lambda z: z.reshape(1, c).astype(F32)
    sp = jax.nn.softplus(-lam)
    return pl.pallas_call(
        functools.partial(_lru_kernel, tm=tm, nq=nq),
        grid=(bsz, nt),
        in_specs=[
            pl.BlockSpec((tm, c), lambda b, i: (b * nt + i, bx)),
            pl.BlockSpec((SUBLANES, c), lambda b, i: (jnp.maximum((b * nt + i) * rpb - 1, 0), bx)),
            pl.BlockSpec((tm, c), lambda b, i: (b * nt + i, by)),
            pl.BlockSpec((CONV_W, c), lambda b, i: (0, 0)),
            pl.BlockSpec((1, c), lambda b, i: (0, 0)),
            pl.BlockSpec((nq, per * bw, per * bw), lambda b, i: (0, 0, 0)),
            pl.BlockSpec((1, c), lambda b, i: (0, 0)),
            pl.BlockSpec((nq, per * bw, per * bw), lambda b, i: (0, 0, 0)),
            pl.BlockSpec((1, c), lambda b, i: (0, 0)),
            pl.BlockSpec((1, c), lambda b, i: (0, 0)),
        ],
        out_specs=pl.BlockSpec((tm, c), lambda b, i: (b * nt + i, 0)),
        out_shape=jax.ShapeDtypeStruct((t, c), BF16),
        scratch_shapes=[pltpu.VMEM((1, c), F32)],
        compiler_params=_cp(("parallel", "arbitrary")),
        name="lru",
    )(p_a, p_a, p_a, conv_w.astype(F32), vec(conv_b), blockdiag(w_r), vec(b_r), blockdiag(w_i), vec(b_i),
      vec(sp))


def _xattn_kernel(q_ref, k_ref, v_ref, o_ref, *, hd):
    scale = hd ** -0.5
    outs = []
    for h in range(XA_HEADS):
        sl = slice(h * hd, (h + 1) * hd)
        s = _mm_nt(q_ref[:, sl], k_ref[:, sl]) * scale
        m = jnp.max(s, axis=-1, keepdims=True)
        e = jnp.exp(s - m)
        pr = e / jnp.sum(e, axis=-1, keepdims=True)
        outs.append(_mm(pr.astype(BF16), v_ref[:, sl]))
    o_ref[...] = jnp.concatenate(outs, axis=1).astype(o_ref.dtype)


def xattn_branch(p_b, kv, bsz, seq, tm):
    t = p_b.shape[0]
    c = kv.shape[1] // 2
    mlen = kv.shape[0] // bsz
    nt = seq // tm
    return pl.pallas_call(
        functools.partial(_xattn_kernel, hd=c // XA_HEADS),
        grid=(bsz, nt),
        in_specs=[
            pl.BlockSpec((tm, c), lambda b, i: (b * nt + i, 0)),
            pl.BlockSpec((mlen, c), lambda b, i: (b, 0)),
            pl.BlockSpec((mlen, c), lambda b, i: (b, 1)),
        ],
        out_specs=pl.BlockSpec((tm, c), lambda b, i: (b * nt + i, 0)),
        out_shape=jax.ShapeDtypeStruct((t, c), BF16),
        compiler_params=_cp(("parallel", "parallel")),
        name="xattn",
    )(p_b, kv, kv)


def _merge_kernel(ya_ref, yb_ref, yc_ref, wa_ref, wb_ref, wc_ref, ga_ref, gb_ref, gc_ref, o_ref):
    acc = _sigmoid(ga_ref[...].astype(F32)) * _mm(ya_ref[...], wa_ref[...])
    acc = acc + _sigmoid(gb_ref[...].astype(F32)) * _mm(yb_ref[...], wb_ref[...])
    acc = acc + _sigmoid(gc_ref[...].astype(F32)) * _mm(yc_ref[...], wc_ref[...])
    o_ref[...] = acc.astype(o_ref.dtype)


def merge(y_a, y_b, y_c, w_a, w_b, w_c, p_b, gate_col0, tm, tn):
    t, c = y_a.shape
    d = w_a.shape[1]
    g0 = gate_col0 // tn
    gstep = d // tn
    assert gate_col0 % tn == 0
    yspec = pl.BlockSpec((tm, c), lambda i, j: (i, 0))
    wspec = pl.BlockSpec((c, tn), lambda i, j: (0, j))
    gspec = lambda n: pl.BlockSpec((tm, tn), lambda i, j: (i, g0 + n * gstep + j))
    return pl.pallas_call(
        _merge_kernel,
        grid=(t // tm, d // tn),
        in_specs=[yspec, yspec, yspec, wspec, wspec, wspec, gspec(0), gspec(1), gspec(2)],
        out_specs=pl.BlockSpec((tm, tn), lambda i, j: (i, j)),
        out_shape=jax.ShapeDtypeStruct((t, d), BF16),
        compiler_params=_cp(("parallel", "parallel")),
        name="merge",
    )(y_a, y_b, y_c, w_a, w_b, w_c, p_b, p_b, p_b)


def _wo_kernel(x_ref, m_ref, wo_ref, g_ref, wrh_ref, wrl_ref, br_ref, h_ref, xn_ref, lg_ref):
    h = x_ref[...].astype(F32) + _mm(m_ref[...], wo_ref[...])
    h_ref[...] = h
    ms = jnp.mean(h * h, axis=-1, keepdims=True)
    xn = h * lax.rsqrt(ms + EPS) * g_ref[...]
    xn_ref[...] = xn
    hi, lo = _split(xn)
    lg_ref[...] = _mm(hi, wrh_ref[...]) + _mm(lo, wrh_ref[...]) + _mm(hi, wrl_ref[...]) + br_ref[...]


def wo_block(x, merged, w_o, g_ffn, w_router, b_router, tm):
    t, d = x.shape
    ne = w_router.shape[1]
    wr = jnp.pad(w_router.astype(F32), ((0, 0), (0, LANES - ne)))
    wr_hi = wr.astype(BF16)
    wr_lo = (wr - wr_hi.astype(F32)).astype(BF16)
    br = jnp.pad(b_router.astype(F32), (0, LANES - ne), constant_values=NEG_BIG).reshape(1, LANES)
    row = pl.BlockSpec((tm, d), lambda i: (i, 0))
    full = lambda r, c: pl.BlockSpec((r, c), lambda i: (0, 0))
    return pl.pallas_call(
        _wo_kernel,
        grid=(t // tm,),
        in_specs=[row, row, full(d, d), full(1, d), full(d, LANES), full(d, LANES), full(1, LANES)],
        out_specs=[row, row, pl.BlockSpec((tm, LANES), lambda i: (i, 0))],
        out_shape=[jax.ShapeDtypeStruct((t, d), F32), jax.ShapeDtypeStruct((t, d), F32),
                   jax.ShapeDtypeStruct((t, LANES), F32)],
        compiler_params=_cp(("parallel",)),
        name="wo",
    )(x, merged, w_o.astype(BF16), g_ffn.reshape(1, d).astype(F32), wr_hi, wr_lo, br)


def _router_kernel(lg_ref, idx_ref, gate_ref, rank_ref, cnt_ref, carry_ref, *, tm):
    i = pl.program_id(0)

    @pl.when(i == 0)
    def _():
        carry_ref[...] = jnp.zeros_like(carry_ref)

    l = lg_ref[...]
    lane = lax.broadcasted_iota(jnp.int32, l.shape, 1).astype(F32)
    vals, sels, idxs = [], [], []
    onehot = jnp.zeros(l.shape, F32)
    for _ in range(TOP_K):
        m = jnp.max(l, axis=-1, keepdims=True)
        idx = jnp.min(jnp.where(l == m, lane, float(LANES)), axis=-1, keepdims=True)
        sel = lane == idx
        vals.append(m)
        sels.append(sel)
        idxs.append(idx)
        onehot = onehot + jnp.where(sel, 1.0, 0.0)
        l = jnp.where(sel, -jnp.inf, l)
    es = [jnp.exp(vv - vals[0]) for vv in vals]
    den = es[0] + es[1] + es[2] + es[3]

    trow = lax.broadcasted_iota(jnp.int32, (tm, tm), 0)
    tcol = lax.broadcasted_iota(jnp.int32, (tm, tm), 1)
    tri = jnp.where(tcol < trow, 1.0, 0.0).astype(BF16)
    cum = _mm(tri, onehot.astype(BF16)) + carry_ref[...]
    carry = carry_ref[...] + jnp.sum(onehot, axis=0, keepdims=True)
    carry_ref[...] = carry
    cnt_ref[...] = carry

    idx_out = jnp.zeros(l.shape, F32)
    gate_out = jnp.zeros(l.shape, F32)
    rank_out = jnp.zeros(l.shape, F32)
    for kk in range(TOP_K):
        rk = jnp.sum(jnp.where(sels[kk], cum, 0.0), axis=-1, keepdims=True)
        idx_out = jnp.where(lane == kk, idxs[kk], idx_out)
        gate_out = jnp.where(lane == kk, es[kk] / den, gate_out)
        rank_out = jnp.where(lane == kk, rk, rank_out)
    idx_ref[...] = idx_out.astype(jnp.int32)
    gate_ref[...] = gate_out
    rank_ref[...] = rank_out.astype(jnp.int32)


def router(logits, tm):
    t = logits.shape[0]
    spec = pl.BlockSpec((tm, LANES), lambda i: (i, 0))
    return pl.pallas_call(
        functools.partial(_router_kernel, tm=tm),
        grid=(t // tm,),
        in_specs=[spec],
        out_specs=[spec, spec, spec, pl.BlockSpec((1, LANES), lambda i: (0, 0))],
        out_shape=[jax.ShapeDtypeStruct((t, LANES), jnp.int32), jax.ShapeDtypeStruct((t, LANES), F32),
                   jax.ShapeDtypeStruct((t, LANES), jnp.int32), jax.ShapeDtypeStruct((1, LANES), F32)],
        scratch_shapes=[pltpu.VMEM((1, LANES), F32)],
        compiler_params=_cp(("arbitrary",)),
        name="router",
    )(logits)


def _dispatch_kernel(dest_ref, zrow_ref, meta_ref, x_ref, rows_ref, zero_ref, zsem, sem, *, tm, ne, n_sub_total):
    i = pl.program_id(0)

    @pl.when(i == 0)
    def _():
        zero_ref[...] = jnp.zeros_like(zero_ref)

        def zcopy(row0):
            return pltpu.make_async_copy(
                zero_ref, rows_ref.at[pl.ds(pl.multiple_of(row0, SUB_ROWS), SUB_ROWS), :], zsem)

        for e in range(ne):
            zcopy(zrow_ref[e]).start()

        def tail(sb, carry):
            zcopy(sb * SUB_ROWS).start()
            return carry

        lax.fori_loop(meta_ref[0], n_sub_total, tail, 0)
        for e in range(ne):
            zcopy(zrow_ref[e]).wait()

        def tail_wait(sb, carry):
            zcopy(sb * SUB_ROWS).wait()
            return carry

        lax.fori_loop(meta_ref[0], n_sub_total, tail_wait, 0)

    base = i * (tm * TOP_K)

    def row_copy(r, dst_row):
        return pltpu.make_async_copy(x_ref.at[pl.ds(r, 1), :], rows_ref.at[pl.ds(dst_row, 1), :], sem)

    for r in range(tm):
        for kk in range(TOP_K):
            row_copy(r, dest_ref[base + (r * TOP_K + kk)]).start()
    for kk in range(TOP_K):
        pltpu.make_async_copy(x_ref, rows_ref.at[pl.ds(0, tm), :], sem).wait()


def dispatch(xn, dest_flat, zrow, used_sub, n_rows, tm):
    t, d = xn.shape
    ne = zrow.shape[0]
    grid_spec = pltpu.PrefetchScalarGridSpec(
        num_scalar_prefetch=3,
        grid=(t // tm,),
        in_specs=[pl.BlockSpec((tm, d), lambda i, *_: (i, 0))],
        out_specs=pl.BlockSpec(memory_space=pl.ANY),
        scratch_shapes=[pltpu.VMEM((SUB_ROWS, d), xn.dtype), pltpu.SemaphoreType.DMA, pltpu.SemaphoreType.DMA],
    )
    return pl.pallas_call(
        functools.partial(_dispatch_kernel, tm=tm, ne=ne, n_sub_total=n_rows // SUB_ROWS),
        grid_spec=grid_spec,
        out_shape=jax.ShapeDtypeStruct((n_rows, d), xn.dtype),
        compiler_params=_cp(("arbitrary",)),
        name="dispatch",
    )(dest_flat, zrow, used_sub, xn)


def _expert_kernel(ge_ref, r0_ref, ns_ref, meta_ref, rows_ref, wg_ref, bg_ref, wu_ref, bu_ref, wd_ref, bd_ref,
                   out_ref, stage_ref, xb_ref, acc_ref, ost_ref, xsem, osem, *, n_sub_total):
    g = pl.program_id(0)
    j = pl.program_id(1)
    nj = pl.num_programs(1)
    ng = meta_ref[1]
    used = g < ng
    nsub = ns_ref[g]
    d = stage_ref.shape[1]

    def x_copy(grp, sidx):
        row0 = pl.multiple_of(r0_ref[grp] + sidx * SUB_ROWS, SUB_ROWS)
        return pltpu.make_async_copy(rows_ref.at[pl.ds(row0, SUB_ROWS), :],
                                     stage_ref.at[pl.ds(sidx * SUB_ROWS, SUB_ROWS), :], xsem)

    def o_copy(row0, sidx):
        return pltpu.make_async_copy(ost_ref.at[pl.ds(sidx * SUB_ROWS, SUB_ROWS), :],
                                     out_ref.at[pl.ds(pl.multiple_of(row0, SUB_ROWS), SUB_ROWS), :], osem)

    def for_subs(count, fn):
        for sidx in range(GROUP_SUBS):
            @pl.when(sidx < count)
            def _():
                fn(sidx)

    @pl.when((g == 0) & (j == 0))
    def _():
        for_subs(ns_ref[0], lambda sidx: x_copy(0, sidx).start())
        ost_ref[0:SUB_ROWS, :] = jnp.zeros((SUB_ROWS, d), F32)

        def tail(sb, carry):
            cp = o_copy(sb * SUB_ROWS, 0)
            cp.start()
            cp.wait()
            return carry

        lax.fori_loop(meta_ref[0], n_sub_total, tail, 0)

    @pl.when(used & (j == 0))
    def _():
        for_subs(nsub, lambda sidx: x_copy(g, sidx).wait())

        def stage_in(sidx):
            sl = pl.ds(sidx * SUB_ROWS, SUB_ROWS)
            xb_ref[sl, :] = stage_ref[sl, :].astype(BF16)
            acc_ref[sl, :] = jnp.zeros((SUB_ROWS, d), F32)

        for_subs(nsub, stage_in)

        @pl.when(g + 1 < ng)
        def _():
            for_subs(ns_ref[g + 1], lambda sidx: x_copy(g + 1, sidx).start())

    @pl.when(used)
    def _():
        for n in range(1, GROUP_SUBS + 1):
            @pl.when(nsub == n)
            def _():
                rows = n * SUB_ROWS
                xb = xb_ref[0:rows, :]
                hg = jnp.minimum(_mm(xb, wg_ref[...].astype(BF16)) + bg_ref[...], SWIGLU_LIMIT)
                hu = jnp.clip(_mm(xb, wu_ref[...].astype(BF16)) + bu_ref[...], -SWIGLU_LIMIT, SWIGLU_LIMIT)
                hh = (hu + 1.0) * hg * _sigmoid(SWIGLU_ALPHA * hg)
                acc_ref[0:rows, :] += _mm(hh.astype(BF16), wd_ref[...].astype(BF16))

    @pl.when(used & (j == nj - 1))
    def _():
        @pl.when(g > 0)
        def _():
            for_subs(ns_ref[g - 1], lambda sidx: o_copy(r0_ref[g - 1] + sidx * SUB_ROWS, sidx).wait())

        def result_out(sidx):
            sl = pl.ds(sidx * SUB_ROWS, SUB_ROWS)
            ost_ref[sl, :] = acc_ref[sl, :] + bd_ref[...]
            o_copy(r0_ref[g] + sidx * SUB_ROWS, sidx).start()

        for_subs(nsub, result_out)

        @pl.when(g == ng - 1)
        def _():
            for_subs(nsub, lambda sidx: o_copy(r0_ref[g] + sidx * SUB_ROWS, sidx).wait())


def experts(rows, g_expert, g_row0, g_nsub, meta, w_gate, b_gate, w_up, b_up, w_down, b_down, tn):
    n_rows, d = rows.shape
    ne, _, de = w_gate.shape
    n_groups_max = g_expert.shape[0]
    nj = de // tn
    gr = GROUP_SUBS * SUB_ROWS

    def eb(g, ge, meta):
        return ge[jnp.minimum(g, meta[1] - 1)]

    def jb(g, j, meta):
        return jnp.where(g < meta[1], j, nj - 1)

    grid_spec = pltpu.PrefetchScalarGridSpec(
        num_scalar_prefetch=4,
        grid=(n_groups_max, nj),
        in_specs=[
            pl.BlockSpec(memory_space=pl.ANY),
            pl.BlockSpec((None, d, tn), lambda g, j, ge, r0, ns, mt: (eb(g, ge, mt), 0, jb(g, j, mt))),
            pl.BlockSpec((None, 1, tn), lambda g, j, ge, r0, ns, mt: (eb(g, ge, mt), 0, jb(g, j, mt))),
            pl.BlockSpec((None, d, tn), lambda g, j, ge, r0, ns, mt: (eb(g, ge, mt), 0, jb(g, j, mt))),
            pl.BlockSpec((None, 1, tn), lambda g, j, ge, r0, ns, mt: (eb(g, ge, mt), 0, jb(g, j, mt))),
            pl.BlockSpec((None, tn, d), lambda g, j, ge, r0, ns, mt: (eb(g, ge, mt), jb(g, j, mt), 0)),
            pl.BlockSpec((None, 1, d), lambda g, j, ge, r0, ns, mt: (eb(g, ge, mt), 0, 0)),
        ],
        out_specs=pl.BlockSpec(memory_space=pl.ANY),
        scratch_shapes=[
            pltpu.VMEM((gr, d), F32),
            pltpu.VMEM((gr, d), BF16),
            pltpu.VMEM((gr, d), F32),
            pltpu.VMEM((gr, d), F32),
            pltpu.SemaphoreType.DMA,
            pltpu.SemaphoreType.DMA,
        ],
    )
    return pl.pallas_call(
        functools.partial(_expert_kernel, n_sub_total=n_rows // SUB_ROWS),
        grid_spec=grid_spec,
        out_shape=jax.ShapeDtypeStruct((n_rows, d), F32),
        compiler_params=_cp(("arbitrary", "arbitrary")),
        name="experts",
    )(g_expert, g_row0, g_nsub, meta, rows, w_gate, b_gate.reshape(ne, 1, de), w_up, b_up.reshape(ne, 1, de),
      w_down, b_down.reshape(ne, 1, d))


def _combine_kernel(dest_ref, rows_ref, h_ref, gate_ref, g_ref, o_ref, buf_ref, sem, *, tm):
    i = pl.program_id(0)
    nt = pl.num_programs(0)
    n = tm * TOP_K
    slot = i % 2

    def issue_tile(tile, slt):
        base = tile * n
        for r in range(tm):
            for kk in range(TOP_K):
                pltpu.make_async_copy(rows_ref.at[pl.ds(dest_ref[base + (r * TOP_K + kk)], 1), :],
                                      buf_ref.at[slt, pl.ds(kk * tm + r, 1), :], sem.at[slt]).start()

    @pl.when(i == 0)
    def _():
        issue_tile(0, 0)

    for par in range(2):
        @pl.when(slot == par)
        def _():
            pltpu.make_async_copy(rows_ref.at[pl.ds(0, n), :], buf_ref.at[par], sem.at[par]).wait()

            @pl.when(i + 1 < nt)
            def _():
                issue_tile(i + 1, 1 - par)

    gates = gate_ref[...]
    h = h_ref[...]
    for kk in range(TOP_K):
        h = h + gates[:, kk:kk + 1] * buf_ref[slot, kk * tm:(kk + 1) * tm, :]
    ms = jnp.mean(h * h, axis=-1, keepdims=True)
    o_ref[...] = (h * lax.rsqrt(ms + EPS) * g_ref[...]).astype(o_ref.dtype)


def combine(dest_flat, out_rows, h1, gates, g_final, tm, out_dtype):
    t, d = h1.shape
    grid_spec = pltpu.PrefetchScalarGridSpec(
        num_scalar_prefetch=1,
        grid=(t // tm,),
        in_specs=[
            pl.BlockSpec(memory_space=pl.ANY),
            pl.BlockSpec((tm, d), lambda i, *_: (i, 0)),
            pl.BlockSpec((tm, LANES), lambda i, *_: (i, 0)),
            pl.BlockSpec((1, d), lambda i, *_: (0, 0)),
        ],
        out_specs=pl.BlockSpec((tm, d), lambda i, *_: (i, 0)),
        scratch_shapes=[pltpu.VMEM((2, tm * TOP_K, d), out_rows.dtype), pltpu.SemaphoreType.DMA((2,))],
    )
    return pl.pallas_call(
        functools.partial(_combine_kernel, tm=tm),
        grid_spec=grid_spec,
        out_shape=jax.ShapeDtypeStruct((t, d), out_dtype),
        compiler_params=_cp(("arbitrary",)),
        name="combine",
    )(dest_flat, out_rows, h1, gates, g_final.reshape(1, d).astype(F32))


def _tile(n, pref):
    while n % pref:
        pref //= 2
    return pref


def _group_tables(counts, n_groups_max):
    ne = counts.shape[0]
    nsub_e = (counts + SUB_ROWS - 1) // SUB_ROWS
    pend_sub = jnp.cumsum(nsub_e)
    pstart = (pend_sub - nsub_e) * SUB_ROWS
    ng_e = (nsub_e + GROUP_SUBS - 1) // GROUP_SUBS
    base_e = nsub_e // jnp.maximum(ng_e, 1)
    rem_e = nsub_e - base_e * ng_e
    cum_ng = jnp.cumsum(ng_e)
    n_groups = cum_ng[-1]
    g = jnp.arange(n_groups_max, dtype=jnp.int32)
    e_g = jnp.minimum(jnp.sum((g[:, None] >= cum_ng[None, :]).astype(jnp.int32), axis=1), ne - 1)
    lg = g - (cum_ng[e_g] - ng_e[e_g])
    valid = g < n_groups
    g_nsub = jnp.where(valid, base_e[e_g] + (lg < rem_e[e_g]).astype(jnp.int32), 0)
    g_row0 = jnp.where(valid, pstart[e_g] + SUB_ROWS * (lg * base_e[e_g] + jnp.minimum(lg, rem_e[e_g])), 0)
    zrow = pstart + jnp.maximum(nsub_e - 1, 0) * SUB_ROWS
    meta = jnp.stack([pend_sub[-1], n_groups]).astype(jnp.int32)
    i32 = lambda z: z.astype(jnp.int32)
    return i32(pstart), i32(zrow), i32(e_g), i32(g_row0), i32(g_nsub), meta


def kernel(x, mem, g_mix, w_in, mu_rwkv, rwkv_w0, rwkv_w2, rwkv_a0, rwkv_a2, rwkv_g2, rwkv_k_k, rwkv_k_a,
           rwkv_r_k, rwkv_ln_w, rwkv_ln_b, w_rwkv_out, lru_conv_w, lru_conv_b, lru_w_r, lru_b_r, lru_w_i,
           lru_b_i, lru_lambda, w_lru_out, g_mem, w_mem_k, w_mem_v, w_xa_out, w_o, g_ffn, w_router,
           b_router, w_gate, b_gate, w_up, b_up, w_down, b_down, g_final):
    bsz, seq, d = x.shape
    assert w_in.shape[0] == 1, "single-layer block"
    l = 0
    t = bsz * seq
    c = rwkv_w0.shape[1]
    n_lora = LORA_W + LORA_A + LORA_G
    ne = w_router.shape[2]
    assert lru_conv_b.shape[1] == c and w_mem_k.shape[2] == c and c % LORA_PAD == 0

    h = x.reshape(t, d).astype(F32)
    memf = mem.reshape(bsz * mem.shape[1], d)
    wi = w_in[l].astype(BF16)
    c0 = 3 * c + n_lora
    w_a = jnp.concatenate(
        [wi[:, 0:3 * c], wi[:, c0:c0 + 2 * c], wi[:, 3 * c:c0], jnp.zeros((d, LORA_PAD - n_lora), BF16)], axis=1)
    w_b = wi[:, c0 + 2 * c:]

    xn = norm_cast(h, g_mix[l], _tile(t, 512))
    p_a = matmul(xn, w_a, _tile(t, 1024), _tile(w_a.shape[1], 512), F32)
    p_b = matmul(xn, w_b, _tile(t, 1024), _tile(w_b.shape[1], 512), BF16)
    mn = norm_cast(memf, g_mem[l], _tile(memf.shape[0], 256))
    kv = matmul(mn, jnp.concatenate([w_mem_k[l], w_mem_v[l]], axis=1).astype(BF16),
                _tile(memf.shape[0], 256), _tile(2 * c, 512), BF16)

    y_a = rwkv_branch(p_a, bsz, seq, mu_rwkv[l], rwkv_w0[l], rwkv_w2[l], rwkv_a0[l], rwkv_a2[l],
                      rwkv_g2[l], rwkv_k_k[l], rwkv_k_a[l], rwkv_r_k[l].reshape(-1), rwkv_ln_w[l],
                      rwkv_ln_b[l], _tile(seq, 256), 5 * c)
    y_b = lru_branch(p_a, bsz, seq, 3 * c, 4 * c, lru_conv_w[l], lru_conv_b[l], lru_w_r[l], lru_b_r[l],
                     lru_w_i[l], lru_b_i[l], lru_lambda[l], _tile(seq, 256))
    y_c = xattn_branch(p_b, kv, bsz, seq, _tile(seq, 512))
    merged = merge(y_a, y_b, y_c, w_rwkv_out[l].astype(BF16), w_lru_out[l].astype(BF16),
                   w_xa_out[l].astype(BF16), p_b, c, _tile(t, 512), _tile(d, 512))
    h1, xn2, logits = wo_block(h, merged, w_o[l], g_ffn[l], w_router[l], b_router[l], _tile(t, 256))

    idx, gates, rank, counts = router(logits, _tile(t, 512))
    counts = counts[0, :ne].astype(jnp.int32)
    n_rows = t * TOP_K + ne * SUB_ROWS
    n_groups_max = (t * TOP_K) // (GROUP_SUBS * SUB_ROWS) + ne
    pstart, zrow, g_expert, g_row0, g_nsub, meta = _group_tables(counts, n_groups_max)
    dest = (pstart[idx[:, :TOP_K]] + rank[:, :TOP_K]).reshape(-1)

    tmd = _tile(t, 256)
    rows = dispatch(xn2, dest, zrow, meta, n_rows, tmd)
    out_rows = experts(rows, g_expert, g_row0, g_nsub, meta, w_gate[l], b_gate[l], w_up[l], b_up[l],
                       w_down[l], b_down[l], _tile(w_gate.shape[3], 256))
    out = combine(dest, out_rows, h1, gates, g_final, tmd, x.dtype)
    return out.reshape(bsz, seq, d)
```

```python
import functools

import jax
import jax.numpy as jnp
from jax import lax
from jax.experimental import pallas as pl
from jax.experimental.pallas import tpu as pltpu

F32 = jnp.float32
BF16 = jnp.bfloat16

EPS = 1e-6
GN_EPS = 64e-5
RG_C = 8.0
SWIGLU_LIMIT = 7.0
SWIGLU_ALPHA = 1.702
RWKV_HEAD = 64
LORA_W = 64
LORA_A = 64
LORA_G = 160
CONV_W = 4
LRU_BLOCKS = 16
XA_HEADS = 4
TOP_K = 4
N_BRANCH = 3

LANES = 128
SUBLANES = 8
CHUNK = 64
SUB = 16
SUB_ROWS = 256
GROUP_SUBS = 4
LORA_PAD = 512
NEG_BIG = -1e30
VMEM_LIMIT = 56 * 1024 * 1024


def _cp(sem, vmem=None):
    return pltpu.CompilerParams(dimension_semantics=sem, vmem_limit_bytes=vmem or VMEM_LIMIT)


def _mm(a, b):
    return jnp.dot(a, b, preferred_element_type=F32)


def _mm_nt(a, b):
    return lax.dot_general(a, b, (((1,), (1,)), ((), ())), preferred_element_type=F32)


def _mm_tn(a, b):
    return lax.dot_general(a, b, (((0,), (0,)), ((), ())), preferred_element_type=F32)


def _split(x):
    hi = x.astype(BF16)
    lo = (x - hi.astype(F32)).astype(BF16)
    return hi, lo


def _mm_exact_rhs(x, m):
    hi, lo = _split(x)
    return _mm(hi, m) + _mm(lo, m)


def _mm_exact_lhs(m, x):
    hi, lo = _split(x)
    return _mm(m, hi) + _mm(m, lo)


def _sigmoid(x):
    return 1.0 / (1.0 + jnp.exp(-x))


def _softplus(x):
    return jnp.maximum(x, 0.0) + jnp.log(1.0 + jnp.exp(-jnp.abs(x)))


def _norm_kernel(x_ref, g_ref, o_ref):
    x = x_ref[...].astype(F32)
    ms = jnp.mean(x * x, axis=-1, keepdims=True)
    o_ref[...] = (x * lax.rsqrt(ms + EPS) * g_ref[...]).astype(o_ref.dtype)


def norm_cast(x, g, tm, out_dtype=BF16):
    t, d = x.shape
    return pl.pallas_call(
        _norm_kernel,
        grid=(t // tm,),
        in_specs=[pl.BlockSpec((tm, d), lambda i: (i, 0)), pl.BlockSpec((1, d), lambda i: (0, 0))],
        out_specs=pl.BlockSpec((tm, d), lambda i: (i, 0)),
        out_shape=jax.ShapeDtypeStruct((t, d), out_dtype),
        compiler_params=_cp(("parallel",)),
        name="norm_cast",
    )(x, g.reshape(1, d).astype(F32))


def _matmul_kernel(a_ref, w_ref, o_ref):
    o_ref[...] = _mm(a_ref[...], w_ref[...]).astype(o_ref.dtype)


def matmul(a, w, tm, tn, out_dtype):
    m, k = a.shape
    n = w.shape[1]
    return pl.pallas_call(
        _matmul_kernel,
        grid=(m // tm, n // tn),
        in_specs=[pl.BlockSpec((tm, k), lambda i, j: (i, 0)), pl.BlockSpec((k, tn), lambda i, j: (0, j))],
        out_specs=pl.BlockSpec((tm, tn), lambda i, j: (i, j)),
        out_shape=jax.ShapeDtypeStruct((m, n), out_dtype),
        compiler_params=_cp(("parallel", "parallel")),
        name="matmul",
    )(a, w)


def _rwkv_kernel(r_ref, k_ref, v_ref, lo_ref, rp_ref, kp_ref, vp_ref, lop_ref,
                 mur_ref, muk_ref, muv_ref, mulo_ref, w0_ref, a0_ref, kk_ref, ka_ref, rk_ref,
                 lnw_ref, lnb_ref, w2_ref, a2_ref, g2_ref, o_ref,
                 s_ref, rpq_ref, y0q_ref, abq_ref, gtq_ref, peq_ref, gq_ref, bnq_ref,
                 ra_ref, va_ref, k2a_ref, ava_ref, bva_ref, lda_ref, ca_ref, ga_ref, bna_ref, *, tm, nb):
    i = pl.program_id(1)
    first = i == 0
    a_refs = (ra_ref, va_ref, k2a_ref, ava_ref, bva_ref, lda_ref, ca_ref, ga_ref, bna_ref)

    @pl.when(first)
    def _():
        for ref in (s_ref, rpq_ref, y0q_ref, abq_ref, gtq_ref, peq_ref, gq_ref, bnq_ref) + a_refs:
            ref[...] = jnp.zeros_like(ref)

    bf = lambda z: z.astype(BF16)
    bs = range(nb)
    n2 = 2 * CHUNK
    nch = tm // CHUNK
    items = [(b, q) for q in range(nch) for b in bs]
    rng = range(len(items))
    inv_n = 1.0 / RWKV_HEAD

    lane = lax.broadcasted_iota(jnp.int32, (LANES, LANES), 1)
    sub = lax.broadcasted_iota(jnp.int32, (LANES, LANES), 0)
    e2 = jnp.where((lane // RWKV_HEAD) == (sub // RWKV_HEAD), 1.0, 0.0).astype(BF16)

    def mix(x_ref, p_ref, mu_ref, b):
        x = x_ref[b]
        prev = jnp.where(first, 0.0, p_ref[b, SUBLANES - 1:SUBLANES, :])
        row = lax.broadcasted_iota(jnp.int32, x.shape, 0)
        sh = jnp.where(row == 0, prev, pltpu.roll(x, 1, axis=0))
        return x + (sh - x) * mu_ref[...]

    def recurrence():
        s = [s_ref[b] for b in bs]
        ys = [[] for _ in bs]
        for n, (b, q) in enumerate(items):
            sb = bf(s[b])
            yst = _mm_nt(rpq_ref[n], sb) + y0q_ref[n]
            s[b] = s[b] * peq_ref[n, 0:1, :] + _mm_nt(sb, abq_ref[n]) + gtq_ref[n]
            ys[b].append(yst[0:CHUNK] + yst[CHUNK:n2])
            yield
        for b in bs:
            s_ref[b] = s[b]
        y = [jnp.concatenate(z, axis=0) if len(z) > 1 else z[0] for z in ys]
        mean = [_mm_exact_rhs(z, e2) * inv_n for z in y]
        yield
        yc = [y[b] - mean[b] for b in bs]
        var = [_mm_exact_rhs(z * z, e2) * inv_n for z in yc]
        yield
        for b in bs:
            yn = yc[b] * lax.rsqrt(var[b] + GN_EPS) * lnw_ref[...] + lnb_ref[...]
            o_ref[b] = ((yn + bnq_ref[b]) * gq_ref[b]).astype(o_ref.dtype)

    def prologue():
        r = [mix(r_ref, rp_ref, mur_ref, b) for b in bs]
        yield
        k = [mix(k_ref, kp_ref, muk_ref, b) for b in bs]
        yield
        v = [mix(v_ref, vp_ref, muv_ref, b) for b in bs]
        yield
        lo = [mix(lo_ref, lop_ref, mulo_ref, b) for b in bs]
        yield
        wl = [_mm(bf(jnp.tanh(z[:, 0:LANES])), w2_ref[...]) for z in lo]
        al = [_mm(bf(z[:, 0:LANES]), a2_ref[...]) for z in lo]
        yield
        g = [_mm(bf(_sigmoid(z[:, LANES:3 * LANES])), g2_ref[...]) for z in lo]
        yield
        ld = [-jnp.exp(-_softplus(-(w0_ref[...] + z)) - 0.5) for z in wl]
        a_sig = [_sigmoid(a0_ref[...] + z) for z in al]
        yield
        kk = [z * kk_ref[...] for z in k]
        kk = [z * lax.rsqrt(jnp.maximum(_mm_exact_rhs(z * z, e2), 1e-24)) for z in kk]
        yield
        k2 = [k[b] * (1.0 + (a_sig[b] - 1.0) * ka_ref[...]) for b in bs]
        bonus = [_mm_exact_rhs(r[b] * k2[b] * rk_ref[...], e2) * v[b] for b in bs]
        yield
        trow = lax.broadcasted_iota(jnp.int32, (tm, tm), 0)
        tcol = lax.broadcasted_iota(jnp.int32, (tm, tm), 1)
        tri = jnp.where(((trow // CHUNK) == (tcol // CHUNK)) & (tcol <= trow), 1.0, 0.0).astype(BF16)
        c = [_mm_exact_lhs(tri, z) for z in ld]
        yield
        for b in bs:
            vals = (r[b], v[b], k2[b], -kk[b], kk[b] * a_sig[b], ld[b], c[b], g[b], bonus[b])
            for ref, val in zip(a_refs, vals):
                ref[b] = val

    r, v, k2, a_vec, b_vec, ld, c, g, bonus = ([ref[b] for b in bs] for ref in a_refs)

    lane_row = lax.broadcasted_iota(jnp.int32, (CHUNK, LANES), 1)
    head_a = lane_row < RWKV_HEAD

    def stack(x):
        return jnp.concatenate([jnp.where(head_a, x, 0.0), jnp.where(head_a, 0.0, x)], axis=0)

    ri = lax.broadcasted_iota(jnp.int32, (n2, n2), 0)
    ci = lax.broadcasted_iota(jnp.int32, (n2, n2), 1)
    strict = ri > ci
    incl = ri >= ci
    blk = (ri // SUB) == (ci // SUB)
    eye = jnp.where(ri == ci, 1.0, 0.0)

    sl = lambda q: slice(q * CHUNK, (q + 1) * CHUNK)
    cs = [c[b][sl(q)] for b, q in items]
    lds = [ld[b][sl(q)] for b, q in items]
    c_ends = [z[CHUNK - 1:CHUNK, :] for z in cs]
    e_pos = [jnp.exp(z) for z in cs]
    e_neg = [jnp.exp(-z) for z in cs]
    e_end = [jnp.exp(ce - z) for ce, z in zip(c_ends, cs)]
    p_end = [jnp.exp(ce) for ce in c_ends]
    rt32 = [stack(r[b][sl(q)] * e_pos[n]) for n, (b, q) in enumerate(items)]
    at = [bf(stack(a_vec[b][sl(q)] * jnp.exp(cs[n] - lds[n]))) for n, (b, q) in enumerate(items)]
    rt = [bf(z) for z in rt32]
    bt = [bf(stack(b_vec[b][sl(q)] * e_neg[n])) for n, (b, q) in enumerate(items)]
    kt = [bf(stack(k2[b][sl(q)] * e_neg[n])) for n, (b, q) in enumerate(items)]
    bh = [bf(stack(b_vec[b][sl(q)] * e_end[n])) for n, (b, q) in enumerate(items)]
    kh = [bf(stack(k2[b][sl(q)] * e_end[n])) for n, (b, q) in enumerate(items)]
    vs = [bf(stack(v[b][sl(q)])) for b, q in items]

    ops = {}

    def chunk_operators():
        gram = [_mm_nt(jnp.concatenate([at[n], rt[n]], axis=0), jnp.concatenate([bt[n], kt[n]], axis=0))
                for n in rng]
        yield
        l_ab = [jnp.where(strict, z[0:n2, 0:n2], 0.0) for z in gram]
        l_ak = [bf(jnp.where(strict, z[0:n2, n2:], 0.0)) for z in gram]
        m_rb = [bf(jnp.where(incl, z[n2:, 0:n2], 0.0)) for z in gram]
        m_rk = [bf(jnp.where(incl, z[n2:, n2:], 0.0)) for z in gram]
        yield
        d = [jnp.where(blk, z, 0.0) for z in l_ab]
        l_off = [bf(l_ab[n] - d[n]) for n in rng]
        db = [bf(z) for z in d]
        p = [eye + z for z in d]
        yield
        d2 = [bf(_mm(z, z)) for z in db]
        yield
        lv = [_mm(l_ak[n], vs[n]) for n in rng]
        yield
        p = [p[n] + _mm(d2[n], bf(p[n])) for n in rng]
        yield
        d4 = [bf(_mm(z, z)) for z in d2]
        yield
        p = [p[n] + _mm(d4[n], bf(p[n])) for n in rng]
        yield
        d8 = [bf(_mm(z, z)) for z in d4]
        yield
        td = [bf(p[n] + _mm(d8[n], bf(p[n]))) for n in rng]
        yield
        mm = [bf(_mm(td[n], l_off[n])) for n in rng]
        yield
        x = [_mm(td[n], jnp.concatenate([at[n], bf(lv[n])], axis=1)) for n in rng]
        yield
        m2 = [bf(_mm(z, z)) for z in mm]
        yield
        x = [x[n] + _mm(mm[n], bf(x[n])) for n in rng]
        yield
        xb = [bf(x[n] + _mm(m2[n], bf(x[n]))) for n in rng]
        wb = [z[:, 0:n2] for z in xb]
        u0b = [z[:, n2:] for z in xb]
        yield
        t1 = [_mm(m_rb[n], xb[n]) for n in rng]
        yield
        rp = [bf(rt32[n] + t1[n][:, 0:n2]) for n in rng]
        y0 = [t1[n][:, n2:] + _mm(m_rk[n], vs[n]) for n in rng]
        yield
        ab = [bf(_mm_tn(bh[n], wb[n])) for n in rng]
        yield
        gt = [_mm_tn(u0b[n], bh[n]) + _mm_tn(vs[n], kh[n]) for n in rng]
        ops.update(rp=rp, y0=y0, ab=ab, gt=gt)

    gens = [[chunk_operators(), 19, 0], [prologue(), 10, 0], [recurrence(), len(items) + 2, 0]]
    while gens:
        entry = min(gens, key=lambda e: e[2] / e[1])
        entry[2] += 1
        if next(entry[0], "done") == "done":
            gens.remove(entry)

    for n in rng:
        rpq_ref[n] = ops["rp"][n]
        y0q_ref[n] = ops["y0"][n]
        abq_ref[n] = ops["ab"][n]
        gtq_ref[n] = ops["gt"][n]
        peq_ref[n] = jnp.broadcast_to(p_end[n], (SUBLANES, LANES))
    for b in bs:
        gq_ref[b] = g[b]
        bnq_ref[b] = bonus[b]


def rwkv_branch(p_a, bsz, seq, mu, w0, w2, a0, a2, g2, k_k, k_a, r_k, ln_w, ln_b, tm, col0_lora):
    t, na = p_a.shape
    c = w0.shape[0]
    nj = c // LANES
    nt = seq // tm
    lw = LORA_PAD
    lora_blk = col0_lora // lw
    assert col0_lora % lw == 0 and seq % tm == 0 and tm % CHUNK == 0
    p3 = p_a.reshape(bsz, seq, na)

    mu_rkv = mu[:3 * c].reshape(1, 3 * c)
    mu_lo = jnp.pad(mu[3 * c:], (0, lw - (LORA_W + LORA_A + LORA_G))).reshape(1, lw)
    w2p = jnp.pad(w2, ((0, LANES - LORA_W), (0, 0))).astype(BF16)
    a2p = jnp.pad(a2, ((LORA_W, LANES - LORA_W - LORA_A), (0, 0))).astype(BF16)
    g2p = jnp.pad(g2, ((0, 2 * LANES - LORA_G), (0, 0))).astype(BF16)
    vec = lambda z: z.reshape(1, c).astype(F32)

    rpb = tm // SUBLANES

    tile = lambda i: jnp.minimum(i, nt - 1)
    prow = lambda i: jnp.maximum(tile(i) * rpb - 1, 0)

    def cur(off):
        return pl.BlockSpec((bsz, tm, LANES), lambda j, i: (0, tile(i), off * nj + j))

    def prev(off):
        return pl.BlockSpec((bsz, SUBLANES, LANES), lambda j, i: (0, prow(i), off * nj + j))

    def slab(rows):
        return pl.BlockSpec((rows, LANES), lambda j, i: (0, j))

    n_items = bsz * (tm // CHUNK)
    n2 = 2 * CHUNK
    in_specs = [
        cur(0), cur(1), cur(2),
        pl.BlockSpec((bsz, tm, lw), lambda j, i: (0, tile(i), lora_blk)),
        prev(0), prev(1), prev(2),
        pl.BlockSpec((bsz, SUBLANES, lw), lambda j, i: (0, prow(i), lora_blk)),
        pl.BlockSpec((1, LANES), lambda j, i: (0, j)),
        pl.BlockSpec((1, LANES), lambda j, i: (0, nj + j)),
        pl.BlockSpec((1, LANES), lambda j, i: (0, 2 * nj + j)),
        pl.BlockSpec((1, lw), lambda j, i: (0, 0)),
        slab(1), slab(1), slab(1), slab(1), slab(1), slab(1), slab(1),
        slab(LANES), slab(LANES), slab(2 * LANES),
    ]
    out = pl.pallas_call(
        functools.partial(_rwkv_kernel, tm=tm, nb=bsz),
        grid=(nj, nt + 2),
        in_specs=in_specs,
        out_specs=pl.BlockSpec((bsz, tm, LANES), lambda j, i: (0, jnp.maximum(i - 2, 0), j)),
        out_shape=jax.ShapeDtypeStruct((bsz, seq, c), BF16),
        scratch_shapes=[
            pltpu.VMEM((bsz, n2, n2), F32),
            pltpu.VMEM((n_items, n2, n2), BF16),
            pltpu.VMEM((n_items, n2, n2), F32),
            pltpu.VMEM((n_items, n2, n2), BF16),
            pltpu.VMEM((n_items, n2, n2), F32),
            pltpu.VMEM((n_items, SUBLANES, LANES), F32),
            pltpu.VMEM((bsz, tm, LANES), F32),
            pltpu.VMEM((bsz, tm, LANES), F32),
        ] + [pltpu.VMEM((bsz, tm, LANES), F32)] * 9,
        compiler_params=_cp(("parallel", "arbitrary")),
        name="rwkv",
    )(p3, p3, p3, p3, p3, p3, p3, p3,
      mu_rkv, mu_rkv, mu_rkv, mu_lo, vec(w0), vec(a0), vec(k_k), vec(k_a), vec(r_k), vec(ln_w), vec(ln_b),
      w2p, a2p, g2p)
    return out.reshape(t, c)


def _lru_kernel(x_ref, xp_ref, y_ref, cw_ref, cb_ref, wr_ref, br_ref, wi_ref, bi_ref, sp_ref,
                o_ref, h_ref, *, tm, nq):
    i = pl.program_id(1)
    first = i == 0

    @pl.when(first)
    def _():
        h_ref[...] = jnp.zeros_like(h_ref)

    x = x_ref[...]
    prev = jnp.where(first, 0.0, xp_ref[...])
    xx = jnp.concatenate([prev, x], axis=0)
    xc = cb_ref[...] + x * cw_ref[0:1, :]
    for j in range(1, CONV_W):
        xc = xc + xx[SUBLANES - j:SUBLANES - j + tm, :] * cw_ref[j:j + 1, :]

    xcb = xc.astype(BF16)
    qw = x.shape[1] // nq
    gr = jnp.concatenate([_mm(xcb[:, q * qw:(q + 1) * qw], wr_ref[q]) for q in range(nq)], axis=1)
    gi = jnp.concatenate([_mm(xcb[:, q * qw:(q + 1) * qw], wi_ref[q]) for q in range(nq)], axis=1)
    gate_r = _sigmoid(gr + br_ref[...])
    gate_i = _sigmoid(gi + bi_ref[...])
    log_a = -RG_C * gate_r * sp_ref[...]
    a = jnp.exp(log_a)
    mult = jnp.sqrt(jnp.maximum(1.0 - jnp.exp(2.0 * log_a), 0.0))
    row = lax.broadcasted_iota(jnp.int32, x.shape, 0)
    mult = jnp.where(first & (row == 0), 1.0, mult)
    b = mult * gate_i * xc

    rin = row % SUBLANES
    sft = 1
    while sft < SUBLANES:
        keep = rin >= sft
        a_sh = jnp.where(keep, pltpu.roll(a, sft, axis=0), 1.0)
        b_sh = jnp.where(keep, pltpu.roll(b, sft, axis=0), 0.0)
        b = b + a * b_sh
        a = a * a_sh
        sft *= 2
    carry = h_ref[...]
    hs = []
    for gi in range(tm // SUBLANES):
        rows = slice(gi * SUBLANES, (gi + 1) * SUBLANES)
        hg = a[rows] * carry + b[rows]
        carry = hg[SUBLANES - 1:SUBLANES, :]
        hs.append(hg)
    h = jnp.concatenate(hs, axis=0)
    h_ref[...] = carry

    py = y_ref[...]
    gelu = 0.5 * py * (1.0 + jnp.tanh(0.7978845608028654 * (py + 0.044715 * py * py * py)))
    o_ref[...] = (h * gelu).astype(o_ref.dtype)


def lru_branch(p_a, bsz, seq, col_x, col_y, conv_w, conv_b, w_r, b_r, w_i, b_i, lam, tm):
    t = p_a.shape[0]
    c = conv_b.shape[0]
    nt = seq // tm
    rpb = tm // SUBLANES
    bx, by = col_x // c, col_y // c
    assert col_x % c == 0 and col_y % c == 0
    nblk, bw = w_r.shape[0], w_r.shape[1]
    per = max(1, (2 * LANES) // bw)
    nq = nblk // per

    def blockdiag(w):
        w = w.reshape(nq, per, bw, bw)
        eye = jnp.eye(per, dtype=w.dtype)
        return jnp.einsum('qpab,pr->qparb', w, eye).reshape(nq, per * bw, per * bw).astype(BF16)

    vec = lambda z: z.reshape(1, c).astype(F32)
    sp = jax.nn.softplus(-lam)
    return pl.pallas_call(
        functools.partial(_lru_kernel, tm=tm, nq=nq),
        grid=(bsz, nt),
        in_specs=[
            pl.BlockSpec((tm, c), lambda b, i: (b * nt + i, bx)),
            pl.BlockSpec((SUBLANES, c), lambda b, i: (jnp.maximum((b * nt + i) * rpb - 1, 0), bx)),
            pl.BlockSpec((tm, c), lambda b, i: (b * nt + i, by)),
            pl.BlockSpec((CONV_W, c), lambda b, i: (0, 0)),
            pl.BlockSpec((1, c), lambda b, i: (0, 0)),
            pl.BlockSpec((nq, per * bw, per * bw), lambda b, i: (0, 0, 0)),
            pl.BlockSpec((1, c), lambda b, i: (0, 0)),
            pl.BlockSpec((nq, per * bw, per * bw), lambda b, i: (0, 0, 0)),
            pl.BlockSpec((1, c), lambda b, i: (0, 0)),
            pl.BlockSpec((1, c), lambda b, i: (0, 0)),
        ],
        out_specs=pl.BlockSpec((tm, c), lambda b, i: (b * nt + i, 0)),
        out_shape=jax.ShapeDtypeStruct((t, c), BF16),
        scratch_shapes=[pltpu.VMEM((1, c), F32)],
        compiler_params=_cp(("parallel", "arbitrary")),
        name="lru",
    )(p_a, p_a, p_a, conv_w.astype(F32), vec(conv_b), blockdiag(w_r), vec(b_r), blockdiag(w_i), vec(b_i),
      vec(sp))


def _xattn_kernel(q_ref, k_ref, v_ref, o_ref, *, hd):
    scale = hd ** -0.5
    outs = []
    for h in range(XA_HEADS):
        sl = slice(h * hd, (h + 1) * hd)
        s = _mm_nt(q_ref[:, sl], k_ref[:, sl]) * scale
        m = jnp.max(s, axis=-1, keepdims=True)
        e = jnp.exp(s - m)
        pr = e / jnp.sum(e, axis=-1, keepdims=True)
        outs.append(_mm(pr.astype(BF16), v_ref[:, sl]))
    o_ref[...] = jnp.concatenate(outs, axis=1).astype(o_ref.dtype)


def xattn_branch(p_b, kv, bsz, seq, tm):
    t = p_b.shape[0]
    c = kv.shape[1] // 2
    mlen = kv.shape[0] // bsz
    nt = seq // tm
    return pl.pallas_call(
        functools.partial(_xattn_kernel, hd=c // XA_HEADS),
        grid=(bsz, nt),
        in_specs=[
            pl.BlockSpec((tm, c), lambda b, i: (b * nt + i, 0)),
            pl.BlockSpec((mlen, c), lambda b, i: (b, 0)),
            pl.BlockSpec((mlen, c), lambda b, i: (b, 1)),
        ],
        out_specs=pl.BlockSpec((tm, c), lambda b, i: (b * nt + i, 0)),
        out_shape=jax.ShapeDtypeStruct((t, c), BF16),
        compiler_params=_cp(("parallel", "parallel")),
        name="xattn",
    )(p_b, kv, kv)


def _merge_kernel(ya_ref, yb_ref, yc_ref, wa_ref, wb_ref, wc_ref, ga_ref, gb_ref, gc_ref, o_ref):
    acc = _sigmoid(ga_ref[...].astype(F32)) * _mm(ya_ref[...], wa_ref[...])
    acc = acc + _sigmoid(gb_ref[...].astype(F32)) * _mm(yb_ref[...], wb_ref[...])
    acc = acc + _sigmoid(gc_ref[...].astype(F32)) * _mm(yc_ref[...], wc_ref[...])
    o_ref[...] = acc.astype(o_ref.dtype)


def merge(y_a, y_b, y_c, w_a, w_b, w_c, p_b, gate_col0, tm, tn):
    t, c = y_a.shape
    d = w_a.shape[1]
    g0 = gate_col0 // tn
    gstep = d // tn
    assert gate_col0 % tn == 0
    yspec = pl.BlockSpec((tm, c), lambda i, j: (i, 0))
    wspec = pl.BlockSpec((c, tn), lambda i, j: (0, j))
    gspec = lambda n: pl.BlockSpec((tm, tn), lambda i, j: (i, g0 + n * gstep + j))
    return pl.pallas_call(
        _merge_kernel,
        grid=(t // tm, d // tn),
        in_specs=[yspec, yspec, yspec, wspec, wspec, wspec, gspec(0), gspec(1), gspec(2)],
        out_specs=pl.BlockSpec((tm, tn), lambda i, j: (i, j)),
        out_shape=jax.ShapeDtypeStruct((t, d), BF16),
        compiler_params=_cp(("parallel", "parallel")),
        name="merge",
    )(y_a, y_b, y_c, w_a, w_b, w_c, p_b, p_b, p_b)


def _wo_kernel(x_ref, m_ref, wo_ref, g_ref, wrh_ref, wrl_ref, br_ref, h_ref, xn_ref, lg_ref):
    h = x_ref[...].astype(F32) + _mm(m_ref[...], wo_ref[...])
    h_ref[...] = h
    ms = jnp.mean(h * h, axis=-1, keepdims=True)
    xn = h * lax.rsqrt(ms + EPS) * g_ref[...]
    xn_ref[...] = xn
    hi, lo = _split(xn)
    lg_ref[...] = _mm(hi, wrh_ref[...]) + _mm(lo, wrh_ref[...]) + _mm(hi, wrl_ref[...]) + br_ref[...]


def wo_block(x, merged, w_o, g_ffn, w_router, b_router, tm):
    t, d = x.shape
    ne = w_router.shape[1]
    wr = jnp.pad(w_router.astype(F32), ((0, 0), (0, LANES - ne)))
    wr_hi = wr.astype(BF16)
    wr_lo = (wr - wr_hi.astype(F32)).astype(BF16)
    br = jnp.pad(b_router.astype(F32), (0, LANES - ne), constant_values=NEG_BIG).reshape(1, LANES)
    row = pl.BlockSpec((tm, d), lambda i: (i, 0))
    full = lambda r, c: pl.BlockSpec((r, c), lambda i: (0, 0))
    return pl.pallas_call(
        _wo_kernel,
        grid=(t // tm,),
        in_specs=[row, row, full(d, d), full(1, d), full(d, LANES), full(d, LANES), full(1, LANES)],
        out_specs=[row, row, pl.BlockSpec((tm, LANES), lambda i: (i, 0))],
        out_shape=[jax.ShapeDtypeStruct((t, d), F32), jax.ShapeDtypeStruct((t, d), F32),
                   jax.ShapeDtypeStruct((t, LANES), F32)],
        compiler_params=_cp(("parallel",)),
        name="wo",
    )(x, merged, w_o.astype(BF16), g_ffn.reshape(1, d).astype(F32), wr_hi, wr_lo, br)


def _router_kernel(lg_ref, idx_ref, gate_ref, rank_ref, cnt_ref, carry_ref, *, tm):
    i = pl.program_id(0)

    @pl.when(i == 0)
    def _():
        carry_ref[...] = jnp.zeros_like(carry_ref)

    l = lg_ref[...]
    lane = lax.broadcasted_iota(jnp.int32, l.shape, 1).astype(F32)
    vals, sels, idxs = [], [], []
    onehot = jnp.zeros(l.shape, F32)
    for _ in range(TOP_K):
        m = jnp.max(l, axis=-1, keepdims=True)
        idx = jnp.min(jnp.where(l == m, lane, float(LANES)), axis=-1, keepdims=True)
        sel = lane == idx
        vals.append(m)
        sels.append(sel)
        idxs.append(idx)
        onehot = onehot + jnp.where(sel, 1.0, 0.0)
        l = jnp.where(sel, -jnp.inf, l)
    es = [jnp.exp(vv - vals[0]) for vv in vals]
    den = es[0] + es[1] + es[2] + es[3]

    trow = lax.broadcasted_iota(jnp.int32, (tm, tm), 0)
    tcol = lax.broadcasted_iota(jnp.int32, (tm, tm), 1)
    tri = jnp.where(tcol < trow, 1.0, 0.0).astype(BF16)
    cum = _mm(tri, onehot.astype(BF16)) + carry_ref[...]
    carry = carry_ref[...] + jnp.sum(onehot, axis=0, keepdims=True)
    carry_ref[...] = carry
    cnt_ref[...] = carry

    idx_out = jnp.zeros(l.shape, F32)
    gate_out = jnp.zeros(l.shape, F32)
    rank_out = jnp.zeros(l.shape, F32)
    for kk in range(TOP_K):
        rk = jnp.sum(jnp.where(sels[kk], cum, 0.0), axis=-1, keepdims=True)
        idx_out = jnp.where(lane == kk, idxs[kk], idx_out)
        gate_out = jnp.where(lane == kk, es[kk] / den, gate_out)
        rank_out = jnp.where(lane == kk, rk, rank_out)
    idx_ref[...] = idx_out.astype(jnp.int32)
    gate_ref[...] = gate_out
    rank_ref[...] = rank_out.astype(jnp.int32)


def router(logits, tm):
    t = logits.shape[0]
    spec = pl.BlockSpec((tm, LANES), lambda i: (i, 0))
    return pl.pallas_call(
        functools.partial(_router_kernel, tm=tm),
        grid=(t // tm,),
        in_specs=[spec],
        out_specs=[spec, spec, spec, pl.BlockSpec((1, LANES), lambda i: (0, 0))],
        out_shape=[jax.ShapeDtypeStruct((t, LANES), jnp.int32), jax.ShapeDtypeStruct((t, LANES), F32),
                   jax.ShapeDtypeStruct((t, LANES), jnp.int32), jax.ShapeDtypeStruct((1, LANES), F32)],
        scratch_shapes=[pltpu.VMEM((1, LANES), F32)],
        compiler_params=_cp(("arbitrary",)),
        name="router",
    )(logits)


def _dispatch_kernel(dest_ref, zrow_ref, meta_ref, x_ref, rows_ref, zero_ref, zsem, sem, *, tm, ne, n_sub_total):
    i = pl.program_id(0)

    @pl.when(i == 0)
    def _():
        zero_ref[...] = jnp.zeros_like(zero_ref)

        def zcopy(row0):
            return pltpu.make_async_copy(
                zero_ref, rows_ref.at[pl.ds(pl.multiple_of(row0, SUB_ROWS), SUB_ROWS), :], zsem)

        for e in range(ne):
            zcopy(zrow_ref[e]).start()

        def tail(sb, carry):
            zcopy(sb * SUB_ROWS).start()
            return carry

        lax.fori_loop(meta_ref[0], n_sub_total, tail, 0)
        for e in range(ne):
            zcopy(zrow_ref[e]).wait()

        def tail_wait(sb, carry):
            zcopy(sb * SUB_ROWS).wait()
            return carry

        lax.fori_loop(meta_ref[0], n_sub_total, tail_wait, 0)

    base = i * (tm * TOP_K)

    def row_copy(r, dst_row):
        return pltpu.make_async_copy(x_ref.at[pl.ds(r, 1), :], rows_ref.at[pl.ds(dst_row, 1), :], sem)

    for r in range(tm):
        for kk in range(TOP_K):
            row_copy(r, dest_ref[base + (r * TOP_K + kk)]).start(priority=kk % 2)
    for kk in range(TOP_K):
        pltpu.make_async_copy(x_ref, rows_ref.at[pl.ds(0, tm), :], sem).wait()


def dispatch(xn, dest_flat, zrow, used_sub, n_rows, tm):
    t, d = xn.shape
    ne = zrow.shape[0]
    grid_spec = pltpu.PrefetchScalarGridSpec(
        num_scalar_prefetch=3,
        grid=(t // tm,),
        in_specs=[pl.BlockSpec((tm, d), lambda i, *_: (i, 0))],
        out_specs=pl.BlockSpec(memory_space=pl.ANY),
        scratch_shapes=[pltpu.VMEM((SUB_ROWS, d), xn.dtype), pltpu.SemaphoreType.DMA, pltpu.SemaphoreType.DMA],
    )
    return pl.pallas_call(
        functools.partial(_dispatch_kernel, tm=tm, ne=ne, n_sub_total=n_rows // SUB_ROWS),
        grid_spec=grid_spec,
        out_shape=jax.ShapeDtypeStruct((n_rows, d), xn.dtype),
        compiler_params=_cp(("arbitrary",)),
        name="dispatch",
    )(dest_flat, zrow, used_sub, xn)


def _expert_kernel(ge_ref, r0_ref, ns_ref, meta_ref, rows_ref, wg_ref, bg_ref, wu_ref, bu_ref, wd_ref, bd_ref,
                   out_ref, stage_ref, xb_ref, acc_ref, ost_ref, xsem, osem, *, n_sub_total):
    g = pl.program_id(0)
    j = pl.program_id(1)
    nj = pl.num_programs(1)
    ng = meta_ref[1]
    used = g < ng
    nsub = ns_ref[g]
    d = stage_ref.shape[1]

    def x_copy(grp, sidx):
        row0 = pl.multiple_of(r0_ref[grp] + sidx * SUB_ROWS, SUB_ROWS)
        return pltpu.make_async_copy(rows_ref.at[pl.ds(row0, SUB_ROWS), :],
                                     stage_ref.at[pl.ds(sidx * SUB_ROWS, SUB_ROWS), :], xsem)

    def o_copy(row0, sidx):
        return pltpu.make_async_copy(ost_ref.at[pl.ds(sidx * SUB_ROWS, SUB_ROWS), :],
                                     out_ref.at[pl.ds(pl.multiple_of(row0, SUB_ROWS), SUB_ROWS), :], osem)

    def for_subs(count, fn):
        for sidx in range(GROUP_SUBS):
            @pl.when(sidx < count)
            def _():
                fn(sidx)

    @pl.when((g == 0) & (j == 0))
    def _():
        for_subs(ns_ref[0], lambda sidx: x_copy(0, sidx).start())
        ost_ref[0:SUB_ROWS, :] = jnp.zeros((SUB_ROWS, d), F32)

        def tail(sb, carry):
            cp = o_copy(sb * SUB_ROWS, 0)
            cp.start()
            cp.wait()
            return carry

        lax.fori_loop(meta_ref[0], n_sub_total, tail, 0)

    @pl.when(used & (j == 0))
    def _():
        for_subs(nsub, lambda sidx: x_copy(g, sidx).wait())

        def stage_in(sidx):
            sl = pl.ds(sidx * SUB_ROWS, SUB_ROWS)
            xb_ref[sl, :] = stage_ref[sl, :].astype(BF16)
            acc_ref[sl, :] = jnp.zeros((SUB_ROWS, d), F32)

        for_subs(nsub, stage_in)

        @pl.when(g + 1 < ng)
        def _():
            for_subs(ns_ref[g + 1], lambda sidx: x_copy(g + 1, sidx).start())

    @pl.when(used)
    def _():
        for n in range(1, GROUP_SUBS + 1):
            @pl.when(nsub == n)
            def _():
                rows = n * SUB_ROWS
                xb = xb_ref[0:rows, :]
                hg = jnp.minimum(_mm(xb, wg_ref[...].astype(BF16)) + bg_ref[...], SWIGLU_LIMIT)
                hu = jnp.clip(_mm(xb, wu_ref[...].astype(BF16)) + bu_ref[...], -SWIGLU_LIMIT, SWIGLU_LIMIT)
                hh = (hu + 1.0) * hg * _sigmoid(SWIGLU_ALPHA * hg)
                acc_ref[0:rows, :] += _mm(hh.astype(BF16), wd_ref[...].astype(BF16))

    @pl.when(used & (j == nj - 1))
    def _():
        @pl.when(g > 0)
        def _():
            for_subs(ns_ref[g - 1], lambda sidx: o_copy(r0_ref[g - 1] + sidx * SUB_ROWS, sidx).wait())

        def result_out(sidx):
            sl = pl.ds(sidx * SUB_ROWS, SUB_ROWS)
            ost_ref[sl, :] = acc_ref[sl, :] + bd_ref[...]
            o_copy(r0_ref[g] + sidx * SUB_ROWS, sidx).start()

        for_subs(nsub, result_out)

        @pl.when(g == ng - 1)
        def _():
            for_subs(nsub, lambda sidx: o_copy(r0_ref[g] + sidx * SUB_ROWS, sidx).wait())


def experts(rows, g_expert, g_row0, g_nsub, meta, w_gate, b_gate, w_up, b_up, w_down, b_down, tn):
    n_rows, d = rows.shape
    ne, _, de = w_gate.shape
    n_groups_max = g_expert.shape[0]
    nj = de // tn
    gr = GROUP_SUBS * SUB_ROWS

    def eb(g, ge, meta):
        return ge[jnp.minimum(g, meta[1] - 1)]

    def jb(g, j, meta):
        return jnp.where(g < meta[1], j, nj - 1)

    grid_spec = pltpu.PrefetchScalarGridSpec(
        num_scalar_prefetch=4,
        grid=(n_groups_max, nj),
        in_specs=[
            pl.BlockSpec(memory_space=pl.ANY),
            pl.BlockSpec((None, d, tn), lambda g, j, ge, r0, ns, mt: (eb(g, ge, mt), 0, jb(g, j, mt))),
            pl.BlockSpec((None, 1, tn), lambda g, j, ge, r0, ns, mt: (eb(g, ge, mt), 0, jb(g, j, mt))),
            pl.BlockSpec((None, d, tn), lambda g, j, ge, r0, ns, mt: (eb(g, ge, mt), 0, jb(g, j, mt))),
            pl.BlockSpec((None, 1, tn), lambda g, j, ge, r0, ns, mt: (eb(g, ge, mt), 0, jb(g, j, mt))),
            pl.BlockSpec((None, tn, d), lambda g, j, ge, r0, ns, mt: (eb(g, ge, mt), jb(g, j, mt), 0)),
            pl.BlockSpec((None, 1, d), lambda g, j, ge, r0, ns, mt: (eb(g, ge, mt), 0, 0)),
        ],
        out_specs=pl.BlockSpec(memory_space=pl.ANY),
        scratch_shapes=[
            pltpu.VMEM((gr, d), F32),
            pltpu.VMEM((gr, d), BF16),
            pltpu.VMEM((gr, d), F32),
            pltpu.VMEM((gr, d), F32),
            pltpu.SemaphoreType.DMA,
            pltpu.SemaphoreType.DMA,
        ],
    )
    return pl.pallas_call(
        functools.partial(_expert_kernel, n_sub_total=n_rows // SUB_ROWS),
        grid_spec=grid_spec,
        out_shape=jax.ShapeDtypeStruct((n_rows, d), F32),
        compiler_params=_cp(("arbitrary", "arbitrary")),
        name="experts",
    )(g_expert, g_row0, g_nsub, meta, rows, w_gate, b_gate.reshape(ne, 1, de), w_up, b_up.reshape(ne, 1, de),
      w_down, b_down.reshape(ne, 1, d))


def _combine_kernel(dest_ref, rows_ref, h_ref, gate_ref, g_ref, o_ref, buf_ref, sem, *, tm):
    i = pl.program_id(0)
    nt = pl.num_programs(0)
    n = tm * TOP_K
    slot = i % 2

    def issue_tile(tile, slt):
        base = tile * n
        for r in range(tm):
            for kk in range(TOP_K):
                pltpu.make_async_copy(rows_ref.at[pl.ds(dest_ref[base + (r * TOP_K + kk)], 1), :],
                                      buf_ref.at[slt, pl.ds(kk * tm + r, 1), :], sem.at[slt]
                                      ).start(priority=kk % 2)

    @pl.when(i == 0)
    def _():
        issue_tile(0, 0)

    for par in range(2):
        @pl.when(slot == par)
        def _():
            pltpu.make_async_copy(rows_ref.at[pl.ds(0, n), :], buf_ref.at[par], sem.at[par]).wait()

            @pl.when(i + 1 < nt)
            def _():
                issue_tile(i + 1, 1 - par)

    gates = gate_ref[...]
    h = h_ref[...]
    for kk in range(TOP_K):
        h = h + gates[:, kk:kk + 1] * buf_ref[slot, kk * tm:(kk + 1) * tm, :]
    ms = jnp.mean(h * h, axis=-1, keepdims=True)
    o_ref[...] = (h * lax.rsqrt(ms + EPS) * g_ref[...]).astype(o_ref.dtype)


def combine(dest_flat, out_rows, h1, gates, g_final, tm, out_dtype):
    t, d = h1.shape
    grid_spec = pltpu.PrefetchScalarGridSpec(
        num_scalar_prefetch=1,
        grid=(t // tm,),
        in_specs=[
            pl.BlockSpec(memory_space=pl.ANY),
            pl.BlockSpec((tm, d), lambda i, *_: (i, 0)),
            pl.BlockSpec((tm, LANES), lambda i, *_: (i, 0)),
            pl.BlockSpec((1, d), lambda i, *_: (0, 0)),
        ],
        out_specs=pl.BlockSpec((tm, d), lambda i, *_: (i, 0)),
        scratch_shapes=[pltpu.VMEM((2, tm * TOP_K, d), out_rows.dtype), pltpu.SemaphoreType.DMA((2,))],
    )
    return pl.pallas_call(
        functools.partial(_combine_kernel, tm=tm),
        grid_spec=grid_spec,
        out_shape=jax.ShapeDtypeStruct((t, d), out_dtype),
        compiler_params=_cp(("arbitrary",)),
        name="combine",
    )(dest_flat, out_rows, h1, gates, g_final.reshape(1, d).astype(F32))


def _tile(n, pref):
    while n % pref:
        pref //= 2
    return pref


def _group_tables(counts, n_groups_max):
    ne = counts.shape[0]
    nsub_e = (counts + SUB_ROWS - 1) // SUB_ROWS
    pend_sub = jnp.cumsum(nsub_e)
    pstart = (pend_sub - nsub_e) * SUB_ROWS
    ng_e = (nsub_e + GROUP_SUBS - 1) // GROUP_SUBS
    base_e = nsub_e // jnp.maximum(ng_e, 1)
    rem_e = nsub_e - base_e * ng_e
    cum_ng = jnp.cumsum(ng_e)
    n_groups = cum_ng[-1]
    g = jnp.arange(n_groups_max, dtype=jnp.int32)
    e_g = jnp.minimum(jnp.sum((g[:, None] >= cum_ng[None, :]).astype(jnp.int32), axis=1), ne - 1)
    lg = g - (cum_ng[e_g] - ng_e[e_g])
    valid = g < n_groups
    g_nsub = jnp.where(valid, base_e[e_g] + (lg < rem_e[e_g]).astype(jnp.int32), 0)
    g_row0 = jnp.where(valid, pstart[e_g] + SUB_ROWS * (lg * base_e[e_g] + jnp.minimum(lg, rem_e[e_g])), 0)
    zrow = pstart + jnp.maximum(nsub_e - 1, 0) * SUB_ROWS
    meta = jnp.stack([pend_sub[-1], n_groups]).astype(jnp.int32)
    i32 = lambda z: z.astype(jnp.int32)
    return i32(pstart), i32(zrow), i32(e_g), i32(g_row0), i32(g_nsub), meta


def kernel(x, mem, g_mix, w_in, mu_rwkv, rwkv_w0, rwkv_w2, rwkv_a0, rwkv_a2, rwkv_g2, rwkv_k_k, rwkv_k_a,
           rwkv_r_k, rwkv_ln_w, rwkv_ln_b, w_rwkv_out, lru_conv_w, lru_conv_b, lru_w_r, lru_b_r, lru_w_i,
           lru_b_i, lru_lambda, w_lru_out, g_mem, w_mem_k, w_mem_v, w_xa_out, w_o, g_ffn, w_router,
           b_router, w_gate, b_gate, w_up, b_up, w_down, b_down, g_final):
    bsz, seq, d = x.shape
    assert w_in.shape[0] == 1, "single-layer block"
    l = 0
    t = bsz * seq
    c = rwkv_w0.shape[1]
    n_lora = LORA_W + LORA_A + LORA_G
    ne = w_router.shape[2]
    assert lru_conv_b.shape[1] == c and w_mem_k.shape[2] == c and c % LORA_PAD == 0

    h = x.reshape(t, d).astype(F32)
    memf = mem.reshape(bsz * mem.shape[1], d)
    wi = w_in[l].astype(BF16)
    c0 = 3 * c + n_lora
    w_a = jnp.concatenate(
        [wi[:, 0:3 * c], wi[:, c0:c0 + 2 * c], wi[:, 3 * c:c0], jnp.zeros((d, LORA_PAD - n_lora), BF16)], axis=1)
    w_b = wi[:, c0 + 2 * c:]

    xn = norm_cast(h, g_mix[l], _tile(t, 512))
    p_a = matmul(xn, w_a, _tile(t, 1024), _tile(w_a.shape[1], 512), F32)
    p_b = matmul(xn, w_b, _tile(t, 1024), _tile(w_b.shape[1], 512), BF16)
    mn = norm_cast(memf, g_mem[l], _tile(memf.shape[0], 256))
    kv = matmul(mn, jnp.concatenate([w_mem_k[l], w_mem_v[l]], axis=1).astype(BF16),
                _tile(memf.shape[0], 256), _tile(2 * c, 512), BF16)

    y_a = rwkv_branch(p_a, bsz, seq, mu_rwkv[l], rwkv_w0[l], rwkv_w2[l], rwkv_a0[l], rwkv_a2[l],
                      rwkv_g2[l], rwkv_k_k[l], rwkv_k_a[l], rwkv_r_k[l].reshape(-1), rwkv_ln_w[l],
                      rwkv_ln_b[l], _tile(seq, 256), 5 * c)
    y_b = lru_branch(p_a, bsz, seq, 3 * c, 4 * c, lru_conv_w[l], lru_conv_b[l], lru_w_r[l], lru_b_r[l],
                     lru_w_i[l], lru_b_i[l], lru_lambda[l], _tile(seq, 256))
    y_c = xattn_branch(p_b, kv, bsz, seq, _tile(seq, 512))
    merged = merge(y_a, y_b, y_c, w_rwkv_out[l].astype(BF16), w_lru_out[l].astype(BF16),
                   w_xa_out[l].astype(BF16), p_b, c, _tile(t, 512), _tile(d, 512))
    h1, xn2, logits = wo_block(h, merged, w_o[l], g_ffn[l], w_router[l], b_router[l], _tile(t, 256))

    idx, gates, rank, counts = router(logits, _tile(t, 512))
    counts = counts[0, :ne].astype(jnp.int32)
    n_rows = t * TOP_K + ne * SUB_ROWS
    n_groups_max = (t * TOP_K) // (GROUP_SUBS * SUB_ROWS) + ne
    pstart, zrow, g_expert, g_row0, g_nsub, meta = _group_tables(counts, n_groups_max)
    dest = (pstart[idx[:, :TOP_K]] + rank[:, :TOP_K]).reshape(-1)

    tmd = _tile(t, 256)
    rows = dispatch(xn2, dest, zrow, meta, n_rows, tmd)
    out_rows = experts(rows, g_expert, g_row0, g_nsub, meta, w_gate[l], b_gate[l], w_up[l], b_up[l],
                       w_down[l], b_down[l], _tile(w_gate.shape[3], 256))
    out = combine(dest, out_rows, h1, gates, g_final, tmd, x.dtype)
    return out.reshape(bsz, seq, d)
```

```python
import functools

import jax
import jax.numpy as jnp
from jax import lax
from jax.experimental import pallas as pl
from jax.experimental.pallas import tpu as pltpu

F32 = jnp.float32
BF16 = jnp.bfloat16

EPS = 1e-6
GN_EPS = 64e-5
RG_C = 8.0
SWIGLU_LIMIT = 7.0
SWIGLU_ALPHA = 1.702
RWKV_HEAD = 64
LORA_W = 64
LORA_A = 64
LORA_G = 160
CONV_W = 4
LRU_BLOCKS = 16
XA_HEADS = 4
TOP_K = 4
N_BRANCH = 3

LANES = 128
SUBLANES = 8
CHUNK = 64
SUB = 16
SUB_ROWS = 256
GROUP_SUBS = 4
LORA_PAD = 512
NEG_BIG = -1e30
VMEM_LIMIT = 56 * 1024 * 1024


def _cp(sem, vmem=None):
    return pltpu.CompilerParams(dimension_semantics=sem, vmem_limit_bytes=vmem or VMEM_LIMIT)


def _mm(a, b):
    return jnp.dot(a, b, preferred_element_type=F32)


def _mm_nt(a, b):
    return lax.dot_general(a, b, (((1,), (1,)), ((), ())), preferred_element_type=F32)


def _mm_tn(a, b):
    return lax.dot_general(a, b, (((0,), (0,)), ((), ())), preferred_element_type=F32)


def _split(x):
    hi = x.astype(BF16)
    lo = (x - hi.astype(F32)).astype(BF16)
    return hi, lo


def _mm_exact_rhs(x, m):
    hi, lo = _split(x)
    return _mm(hi, m) + _mm(lo, m)


def _mm_exact_lhs(m, x):
    hi, lo = _split(x)
    return _mm(m, hi) + _mm(m, lo)


def _sigmoid(x):
    return 1.0 / (1.0 + jnp.exp(-x))


def _softplus(x):
    return jnp.maximum(x, 0.0) + jnp.log(1.0 + jnp.exp(-jnp.abs(x)))


def _norm_kernel(x_ref, g_ref, o_ref):
    x = x_ref[...].astype(F32)
    ms = jnp.mean(x * x, axis=-1, keepdims=True)
    o_ref[...] = (x * lax.rsqrt(ms + EPS) * g_ref[...]).astype(o_ref.dtype)


def norm_cast(x, g, tm, out_dtype=BF16):
    t, d = x.shape
    return pl.pallas_call(
        _norm_kernel,
        grid=(t // tm,),
        in_specs=[pl.BlockSpec((tm, d), lambda i: (i, 0)), pl.BlockSpec((1, d), lambda i: (0, 0))],
        out_specs=pl.BlockSpec((tm, d), lambda i: (i, 0)),
        out_shape=jax.ShapeDtypeStruct((t, d), out_dtype),
        compiler_params=_cp(("parallel",)),
        name="norm_cast",
    )(x, g.reshape(1, d).astype(F32))


def _matmul_kernel(a_ref, w_ref, o_ref):
    o_ref[...] = _mm(a_ref[...], w_ref[...]).astype(o_ref.dtype)


def matmul(a, w, tm, tn, out_dtype):
    m, k = a.shape
    n = w.shape[1]
    return pl.pallas_call(
        _matmul_kernel,
        grid=(m // tm, n // tn),
        in_specs=[pl.BlockSpec((tm, k), lambda i, j: (i, 0)), pl.BlockSpec((k, tn), lambda i, j: (0, j))],
        out_specs=pl.BlockSpec((tm, tn), lambda i, j: (i, j)),
        out_shape=jax.ShapeDtypeStruct((m, n), out_dtype),
        compiler_params=_cp(("parallel", "parallel")),
        name="matmul",
    )(a, w)


def _rwkv_kernel(r_ref, k_ref, v_ref, lo_ref, rp_ref, kp_ref, vp_ref, lop_ref,
                 mur_ref, muk_ref, muv_ref, mulo_ref, w0_ref, a0_ref, kk_ref, ka_ref, rk_ref,
                 lnw_ref, lnb_ref, w2_ref, a2_ref, g2_ref, o_ref,
                 s_ref, rpq_ref, y0q_ref, abq_ref, gtq_ref, peq_ref, gq_ref, bnq_ref,
                 ra_ref, va_ref, k2a_ref, ava_ref, bva_ref, lda_ref, ca_ref, ga_ref, bna_ref, *, tm, nb):
    i = pl.program_id(1)
    first = i == 0
    a_refs = (ra_ref, va_ref, k2a_ref, ava_ref, bva_ref, lda_ref, ca_ref, ga_ref, bna_ref)

    @pl.when(first)
    def _():
        for ref in (s_ref, rpq_ref, y0q_ref, abq_ref, gtq_ref, peq_ref, gq_ref, bnq_ref) + a_refs:
            ref[...] = jnp.zeros_like(ref)

    bf = lambda z: z.astype(BF16)
    bs = range(nb)
    n2 = 2 * CHUNK
    nch = tm // CHUNK
    items = [(b, q) for q in range(nch) for b in bs]
    rng = range(len(items))
    inv_n = 1.0 / RWKV_HEAD

    lane = lax.broadcasted_iota(jnp.int32, (LANES, LANES), 1)
    sub = lax.broadcasted_iota(jnp.int32, (LANES, LANES), 0)
    e2 = jnp.where((lane // RWKV_HEAD) == (sub // RWKV_HEAD), 1.0, 0.0).astype(BF16)

    def mix(x_ref, p_ref, mu_ref, b):
        x = x_ref[b]
        prev = jnp.where(first, 0.0, p_ref[b, SUBLANES - 1:SUBLANES, :])
        row = lax.broadcasted_iota(jnp.int32, x.shape, 0)
        sh = jnp.where(row == 0, prev, pltpu.roll(x, 1, axis=0))
        return x + (sh - x) * mu_ref[...]

    def recurrence():
        s = [s_ref[b] for b in bs]
        ys = [[] for _ in bs]
        for n, (b, q) in enumerate(items):
            sb = bf(s[b])
            yst = _mm_nt(rpq_ref[n], sb) + y0q_ref[n]
            s[b] = s[b] * peq_ref[n, 0:1, :] + _mm_nt(sb, abq_ref[n]) + gtq_ref[n]
            ys[b].append(yst[0:CHUNK] + yst[CHUNK:n2])
            yield
        for b in bs:
            s_ref[b] = s[b]
        y = [jnp.concatenate(z, axis=0) if len(z) > 1 else z[0] for z in ys]
        mean = [_mm_exact_rhs(z, e2) * inv_n for z in y]
        yield
        yc = [y[b] - mean[b] for b in bs]
        var = [_mm_exact_rhs(z * z, e2) * inv_n for z in yc]
        yield
        for b in bs:
            yn = yc[b] * lax.rsqrt(var[b] + GN_EPS) * lnw_ref[...] + lnb_ref[...]
            o_ref[b] = ((yn + bnq_ref[b]) * gq_ref[b]).astype(o_ref.dtype)

    def prologue():
        r = [mix(r_ref, rp_ref, mur_ref, b) for b in bs]
        yield
        k = [mix(k_ref, kp_ref, muk_ref, b) for b in bs]
        yield
        v = [mix(v_ref, vp_ref, muv_ref, b) for b in bs]
        yield
        lo = [mix(lo_ref, lop_ref, mulo_ref, b) for b in bs]
        yield
        wl = [_mm(bf(jnp.tanh(z[:, 0:LANES])), w2_ref[...]) for z in lo]
        al = [_mm(bf(z[:, 0:LANES]), a2_ref[...]) for z in lo]
        yield
        g = [_mm(bf(_sigmoid(z[:, LANES:3 * LANES])), g2_ref[...]) for z in lo]
        yield
        ld = [-jnp.exp(-_softplus(-(w0_ref[...] + z)) - 0.5) for z in wl]
        a_sig = [_sigmoid(a0_ref[...] + z) for z in al]
        yield
        kk = [z * kk_ref[...] for z in k]
        kk = [z * lax.rsqrt(jnp.maximum(_mm_exact_rhs(z * z, e2), 1e-24)) for z in kk]
        yield
        k2 = [k[b] * (1.0 + (a_sig[b] - 1.0) * ka_ref[...]) for b in bs]
        bonus = [_mm_exact_rhs(r[b] * k2[b] * rk_ref[...], e2) * v[b] for b in bs]
        yield
        trow = lax.broadcasted_iota(jnp.int32, (tm, tm), 0)
        tcol = lax.broadcasted_iota(jnp.int32, (tm, tm), 1)
        tri = jnp.where(((trow // CHUNK) == (tcol // CHUNK)) & (tcol <= trow), 1.0, 0.0).astype(BF16)
        c = [_mm_exact_lhs(tri, z) for z in ld]
        yield
        for b in bs:
            vals = (r[b], v[b], k2[b], -kk[b], kk[b] * a_sig[b], ld[b], c[b], g[b], bonus[b])
            for ref, val in zip(a_refs, vals):
                ref[b] = val

    r, v, k2, a_vec, b_vec, ld, c, g, bonus = ([ref[b] for b in bs] for ref in a_refs)

    lane_row = lax.broadcasted_iota(jnp.int32, (CHUNK, LANES), 1)
    head_a = lane_row < RWKV_HEAD

    def stack(x):
        return jnp.concatenate([jnp.where(head_a, x, 0.0), jnp.where(head_a, 0.0, x)], axis=0)

    ri = lax.broadcasted_iota(jnp.int32, (n2, n2), 0)
    ci = lax.broadcasted_iota(jnp.int32, (n2, n2), 1)
    strict = ri > ci
    incl = ri >= ci
    blk = (ri // SUB) == (ci // SUB)
    eye = jnp.where(ri == ci, 1.0, 0.0)

    sl = lambda q: slice(q * CHUNK, (q + 1) * CHUNK)
    cs = [c[b][sl(q)] for b, q in items]
    lds = [ld[b][sl(q)] for b, q in items]
    c_ends = [z[CHUNK - 1:CHUNK, :] for z in cs]
    e_pos = [jnp.exp(z) for z in cs]
    e_neg = [jnp.exp(-z) for z in cs]
    e_end = [jnp.exp(ce - z) for ce, z in zip(c_ends, cs)]
    p_end = [jnp.exp(ce) for ce in c_ends]
    rt32 = [stack(r[b][sl(q)] * e_pos[n]) for n, (b, q) in enumerate(items)]
    at = [bf(stack(a_vec[b][sl(q)] * jnp.exp(cs[n] - lds[n]))) for n, (b, q) in enumerate(items)]
    rt = [bf(z) for z in rt32]
    bt = [bf(stack(b_vec[b][sl(q)] * e_neg[n])) for n, (b, q) in enumerate(items)]
    kt = [bf(stack(k2[b][sl(q)] * e_neg[n])) for n, (b, q) in enumerate(items)]
    bh = [bf(stack(b_vec[b][sl(q)] * e_end[n])) for n, (b, q) in enumerate(items)]
    kh = [bf(stack(k2[b][sl(q)] * e_end[n])) for n, (b, q) in enumerate(items)]
    vs = [bf(stack(v[b][sl(q)])) for b, q in items]

    ops = {}

    def chunk_operators():
        gram = [_mm_nt(jnp.concatenate([at[n], rt[n]], axis=0), jnp.concatenate([bt[n], kt[n]], axis=0))
                for n in rng]
        yield
        l_ab = [jnp.where(strict, z[0:n2, 0:n2], 0.0) for z in gram]
        l_ak = [bf(jnp.where(strict, z[0:n2, n2:], 0.0)) for z in gram]
        m_rb = [bf(jnp.where(incl, z[n2:, 0:n2], 0.0)) for z in gram]
        m_rk = [bf(jnp.where(incl, z[n2:, n2:], 0.0)) for z in gram]
        yield
        d = [jnp.where(blk, z, 0.0) for z in l_ab]
        l_off = [bf(l_ab[n] - d[n]) for n in rng]
        db = [bf(z) for z in d]
        p = [eye + z for z in d]
        yield
        d2 = [bf(_mm(z, z)) for z in db]
        yield
        lv = [_mm(l_ak[n], vs[n]) for n in rng]
        yield
        p = [p[n] + _mm(d2[n], bf(p[n])) for n in rng]
        yield
        d4 = [bf(_mm(z, z)) for z in d2]
        yield
        p = [p[n] + _mm(d4[n], bf(p[n])) for n in rng]
        yield
        d8 = [bf(_mm(z, z)) for z in d4]
        yield
        td = [bf(p[n] + _mm(d8[n], bf(p[n]))) for n in rng]
        yield
        mm = [bf(_mm(td[n], l_off[n])) for n in rng]
        yield
        x = [_mm(td[n], jnp.concatenate([at[n], bf(lv[n])], axis=1)) for n in rng]
        yield
        m2 = [bf(_mm(z, z)) for z in mm]
        yield
        x = [x[n] + _mm(mm[n], bf(x[n])) for n in rng]
        yield
        xb = [bf(x[n] + _mm(m2[n], bf(x[n]))) for n in rng]
        wb = [z[:, 0:n2] for z in xb]
        u0b = [z[:, n2:] for z in xb]
        yield
        t1 = [_mm(m_rb[n], xb[n]) for n in rng]
        yield
        rp = [bf(rt32[n] + t1[n][:, 0:n2]) for n in rng]
        y0 = [t1[n][:, n2:] + _mm(m_rk[n], vs[n]) for n in rng]
        yield
        ab = [bf(_mm_tn(bh[n], wb[n])) for n in rng]
        yield
        gt = [_mm_tn(u0b[n], bh[n]) + _mm_tn(vs[n], kh[n]) for n in rng]
        ops.update(rp=rp, y0=y0, ab=ab, gt=gt)

    gens = [[chunk_operators(), 19, 0], [prologue(), 10, 0], [recurrence(), len(items) + 2, 0]]
    while gens:
        entry = min(gens, key=lambda e: e[2] / e[1])
        entry[2] += 1
        if next(entry[0], "done") == "done":
            gens.remove(entry)

    for n in rng:
        rpq_ref[n] = ops["rp"][n]
        y0q_ref[n] = ops["y0"][n]
        abq_ref[n] = ops["ab"][n]
        gtq_ref[n] = ops["gt"][n]
        peq_ref[n] = jnp.broadcast_to(p_end[n], (SUBLANES, LANES))
    for b in bs:
        gq_ref[b] = g[b]
        bnq_ref[b] = bonus[b]


def rwkv_branch(p_a, bsz, seq, mu, w0, w2, a0, a2, g2, k_k, k_a, r_k, ln_w, ln_b, tm, col0_lora):
    t, na = p_a.shape
    c = w0.shape[0]
    nj = c // LANES
    nt = seq // tm
    lw = LORA_PAD
    lora_blk = col0_lora // lw
    assert col0_lora % lw == 0 and seq % tm == 0 and tm % CHUNK == 0
    p3 = p_a.reshape(bsz, seq, na)

    mu_rkv = mu[:3 * c].reshape(1, 3 * c)
    mu_lo = jnp.pad(mu[3 * c:], (0, lw - (LORA_W + LORA_A + LORA_G))).reshape(1, lw)
    w2p = jnp.pad(w2, ((0, LANES - LORA_W), (0, 0))).astype(BF16)
    a2p = jnp.pad(a2, ((LORA_W, LANES - LORA_W - LORA_A), (0, 0))).astype(BF16)
    g2p = jnp.pad(g2, ((0, 2 * LANES - LORA_G), (0, 0))).astype(BF16)
    vec = lambda z: z.reshape(1, c).astype(F32)

    rpb = tm // SUBLANES

    tile = lambda i: jnp.minimum(i, nt - 1)
    prow = lambda i: jnp.maximum(tile(i) * rpb - 1, 0)

    def cur(off):
        return pl.BlockSpec((bsz, tm, LANES), lambda j, i: (0, tile(i), off * nj + j))

    def prev(off):
        return pl.BlockSpec((bsz, SUBLANES, LANES), lambda j, i: (0, prow(i), off * nj + j))

    def slab(rows):
        return pl.BlockSpec((rows, LANES), lambda j, i: (0, j))

    n_items = bsz * (tm // CHUNK)
    n2 = 2 * CHUNK
    in_specs = [
        cur(0), cur(1), cur(2),
        pl.BlockSpec((bsz, tm, lw), lambda j, i: (0, tile(i), lora_blk)),
        prev(0), prev(1), prev(2),
        pl.BlockSpec((bsz, SUBLANES, lw), lambda j, i: (0, prow(i), lora_blk)),
        pl.BlockSpec((1, LANES), lambda j, i: (0, j)),
        pl.BlockSpec((1, LANES), lambda j, i: (0, nj + j)),
        pl.BlockSpec((1, LANES), lambda j, i: (0, 2 * nj + j)),
        pl.BlockSpec((1, lw), lambda j, i: (0, 0)),
        slab(1), slab(1), slab(1), slab(1), slab(1), slab(1), slab(1),
        slab(LANES), slab(LANES), slab(2 * LANES),
    ]
    out = pl.pallas_call(
        functools.partial(_rwkv_kernel, tm=tm, nb=bsz),
        grid=(nj, nt + 2),
        in_specs=in_specs,
        out_specs=pl.BlockSpec((bsz, tm, LANES), lambda j, i: (0, jnp.maximum(i - 2, 0), j)),
        out_shape=jax.ShapeDtypeStruct((bsz, seq, c), BF16),
        scratch_shapes=[
            pltpu.VMEM((bsz, n2, n2), F32),
            pltpu.VMEM((n_items, n2, n2), BF16),
            pltpu.VMEM((n_items, n2, n2), F32),
            pltpu.VMEM((n_items, n2, n2), BF16),
            pltpu.VMEM((n_items, n2, n2), F32),
            pltpu.VMEM((n_items, SUBLANES, LANES), F32),
            pltpu.VMEM((bsz, tm, LANES), F32),
            pltpu.VMEM((bsz, tm, LANES), F32),
        ] + [pltpu.VMEM((bsz, tm, LANES), F32)] * 9,
        compiler_params=_cp(("parallel", "arbitrary")),
        name="rwkv",
    )(p3, p3, p3, p3, p3, p3, p3, p3,
      mu_rkv, mu_rkv, mu_rkv, mu_lo, vec(w0), vec(a0), vec(k_k), vec(k_a), vec(r_k), vec(ln_w), vec(ln_b),
      w2p, a2p, g2p)
    return out.reshape(t, c)


def _lru_kernel(x_ref, xp_ref, y_ref, cw_ref, cb_ref, wr_ref, br_ref, wi_ref, bi_ref, sp_ref,
                o_ref, h_ref, *, tm, nq):
    i = pl.program_id(1)
    first = i == 0

    @pl.when(first)
    def _():
        h_ref[...] = jnp.zeros_like(h_ref)

    x = x_ref[...]
    prev = jnp.where(first, 0.0, xp_ref[...])
    xx = jnp.concatenate([prev, x], axis=0)
    xc = cb_ref[...] + x * cw_ref[0:1, :]
    for j in range(1, CONV_W):
        xc = xc + xx[SUBLANES - j:SUBLANES - j + tm, :] * cw_ref[j:j + 1, :]

    xcb = xc.astype(BF16)
    qw = x.shape[1] // nq
    gr = jnp.concatenate([_mm(xcb[:, q * qw:(q + 1) * qw], wr_ref[q]) for q in range(nq)], axis=1)
    gi = jnp.concatenate([_mm(xcb[:, q * qw:(q + 1) * qw], wi_ref[q]) for q in range(nq)], axis=1)
    gate_r = _sigmoid(gr + br_ref[...])
    gate_i = _sigmoid(gi + bi_ref[...])
    log_a = -RG_C * gate_r * sp_ref[...]
    a = jnp.exp(log_a)
    mult = jnp.sqrt(jnp.maximum(1.0 - jnp.exp(2.0 * log_a), 0.0))
    row = lax.broadcasted_iota(jnp.int32, x.shape, 0)
    mult = jnp.where(first & (row == 0), 1.0, mult)
    b = mult * gate_i * xc

    rin = row % SUBLANES
    sft = 1
    while sft < SUBLANES:
        keep = rin >= sft
        a_sh = jnp.where(keep, pltpu.roll(a, sft, axis=0), 1.0)
        b_sh = jnp.where(keep, pltpu.roll(b, sft, axis=0), 0.0)
        b = b + a * b_sh
        a = a * a_sh
        sft *= 2
    carry = h_ref[...]
    hs = []
    for gi in range(tm // SUBLANES):
        rows = slice(gi * SUBLANES, (gi + 1) * SUBLANES)
        hg = a[rows] * carry + b[rows]
        carry = hg[SUBLANES - 1:SUBLANES, :]
        hs.append(hg)
    h = jnp.concatenate(hs, axis=0)
    h_ref[...] = carry

    py = y_ref[...]
    gelu = 0.5 * py * (1.0 + jnp.tanh(0.7978845608028654 * (py + 0.044715 * py * py * py)))
    o_ref[...] = (h * gelu).astype(o_ref.dtype)


def lru_branch(p_a, bsz, seq, col_x, col_y, conv_w, conv_b, w_r, b_r, w_i, b_i, lam, tm):
    t = p_a.shape[0]
    c = conv_b.shape[0]
    nt = seq // tm
    rpb = tm // SUBLANES
    bx, by = col_x // c, col_y // c
    assert col_x % c == 0 and col_y % c == 0
    nblk, bw = w_r.shape[0], w_r.shape[1]
    per = max(1, (2 * LANES) // bw)
    nq = nblk // per

    def blockdiag(w):
        w = w.reshape(nq, per, bw, bw)
        eye = jnp.eye(per, dtype=w.dtype)
        return jnp.einsum('qpab,pr->qparb', w, eye).reshape(nq, per * bw, per * bw).astype(BF16)

    vec = lambda z: z.reshape(1, c).astype(F32)
    sp = jax.nn.softplus(-lam)
    return pl.pallas_call(
        functools.partial(_lru_kernel, tm=tm, nq=nq),
        grid=(bsz, nt),
        in_specs=[
            pl.BlockSpec((tm, c), lambda b, i: (b * nt + i, bx)),
            pl.BlockSpec((SUBLANES, c), lambda b, i: (jnp.maximum((b * nt + i) * rpb - 1, 0), bx)),
            pl.BlockSpec((tm, c), lambda b, i: (b * nt + i, by)),
            pl.BlockSpec((CONV_W, c), lambda b, i: (0, 0)),
            pl.BlockSpec((1, c), lambda b, i: (0, 0)),
            pl.BlockSpec((nq, per * bw, per * bw), lambda b, i: (0, 0, 0)),
            pl.BlockSpec((1, c), lambda b, i: (0, 0)),
            pl.BlockSpec((nq, per * bw, per * bw), lambda b, i: (0, 0, 0)),
            pl.BlockSpec((1, c), lambda b, i: (0, 0)),
            pl.BlockSpec((1, c), lambda b, i: (0, 0)),
        ],
        out_specs=pl.BlockSpec((tm, c), lambda b, i: (b * nt + i, 0)),
        out_shape=jax.ShapeDtypeStruct((t, c), BF16),
        scratch_shapes=[pltpu.VMEM((1, c), F32)],
        compiler_params=_cp(("parallel", "arbitrary")),
        name="lru",
    )(p_a, p_a, p_a, conv_w.astype(F32), vec(conv_b), blockdiag(w_r), vec(b_r), blockdiag(w_i), vec(b_i),
      vec(sp))


def _xattn_kernel(q_ref, k_ref, v_ref, o_ref, *, hd):
    scale = hd ** -0.5
    outs = []
    for h in range(XA_HEADS):
        sl = slice(h * hd, (h + 1) * hd)
        s = _mm_nt(q_ref[:, sl], k_ref[:, sl]) * scale
        m = jnp.max(s, axis=-1, keepdims=True)
        e = jnp.exp(s - m)
        pr = e / jnp.sum(e, axis=-1, keepdims=True)
        outs.append(_mm(pr.astype(BF16), v_ref[:, sl]))
    o_ref[...] = jnp.concatenate(outs, axis=1).astype(o_ref.dtype)


def xattn_branch(p_b, kv, bsz, seq, tm):
    t = p_b.shape[0]
    c = kv.shape[1] // 2
    mlen = kv.shape[0] // bsz
    nt = seq // tm
    return pl.pallas_call(
        functools.partial(_xattn_kernel, hd=c // XA_HEADS),
        grid=(bsz, nt),
        in_specs=[
            pl.BlockSpec((tm, c), lambda b, i: (b * nt + i, 0)),
            pl.BlockSpec((mlen, c), lambda b, i: (b, 0)),
            pl.BlockSpec((mlen, c), lambda b, i: (b, 1)),
        ],
        out_specs=pl.BlockSpec((tm, c), lambda b, i: (b * nt + i, 0)),
        out_shape=jax.ShapeDtypeStruct((t, c), BF16),
        compiler_params=_cp(("parallel", "parallel")),
        name="xattn",
    )(p_b, kv, kv)


def _merge_kernel(ya_ref, yb_ref, yc_ref, wa_ref, wb_ref, wc_ref, ga_ref, gb_ref, gc_ref, o_ref):
    acc = _sigmoid(ga_ref[...].astype(F32)) * _mm(ya_ref[...], wa_ref[...])
    acc = acc + _sigmoid(gb_ref[...].astype(F32)) * _mm(yb_ref[...], wb_ref[...])
    acc = acc + _sigmoid(gc_ref[...].astype(F32)) * _mm(yc_ref[...], wc_ref[...])
    o_ref[...] = acc.astype(o_ref.dtype)


def merge(y_a, y_b, y_c, w_a, w_b, w_c, p_b, gate_col0, tm, tn):
    t, c = y_a.shape
    d = w_a.shape[1]
    g0 = gate_col0 // tn
    gstep = d // tn
    assert gate_col0 % tn == 0
    yspec = pl.BlockSpec((tm, c), lambda i, j: (i, 0))
    wspec = pl.BlockSpec((c, tn), lambda i, j: (0, j))
    gspec = lambda n: pl.BlockSpec((tm, tn), lambda i, j: (i, g0 + n * gstep + j))
    return pl.pallas_call(
        _merge_kernel,
        grid=(t // tm, d // tn),
        in_specs=[yspec, yspec, yspec, wspec, wspec, wspec, gspec(0), gspec(1), gspec(2)],
        out_specs=pl.BlockSpec((tm, tn), lambda i, j: (i, j)),
        out_shape=jax.ShapeDtypeStruct((t, d), BF16),
        compiler_params=_cp(("parallel", "parallel")),
        name="merge",
    )(y_a, y_b, y_c, w_a, w_b, w_c, p_b, p_b, p_b)


def _wo_kernel(x_ref, m_ref, wo_ref, g_ref, wrh_ref, wrl_ref, br_ref, h_ref, xn_ref, lg_ref):
    h = x_ref[...].astype(F32) + _mm(m_ref[...], wo_ref[...])
    h_ref[...] = h
    ms = jnp.mean(h * h, axis=-1, keepdims=True)
    xn = h * lax.rsqrt(ms + EPS) * g_ref[...]
    xn_ref[...] = xn
    hi, lo = _split(xn)
    lg_ref[...] = _mm(hi, wrh_ref[...]) + _mm(lo, wrh_ref[...]) + _mm(hi, wrl_ref[...]) + br_ref[...]


def wo_block(x, merged, w_o, g_ffn, w_router, b_router, tm):
    t, d = x.shape
    ne = w_router.shape[1]
    wr = jnp.pad(w_router.astype(F32), ((0, 0), (0, LANES - ne)))
    wr_hi = wr.astype(BF16)
    wr_lo = (wr - wr_hi.astype(F32)).astype(BF16)
    br = jnp.pad(b_router.astype(F32), (0, LANES - ne), constant_values=NEG_BIG).reshape(1, LANES)
    row = pl.BlockSpec((tm, d), lambda i: (i, 0))
    full = lambda r, c: pl.BlockSpec((r, c), lambda i: (0, 0))
    return pl.pallas_call(
        _wo_kernel,
        grid=(t // tm,),
        in_specs=[row, row, full(d, d), full(1, d), full(d, LANES), full(d, LANES), full(1, LANES)],
        out_specs=[row, row, pl.BlockSpec((tm, LANES), lambda i: (i, 0))],
        out_shape=[jax.ShapeDtypeStruct((t, d), F32), jax.ShapeDtypeStruct((t, d), F32),
                   jax.ShapeDtypeStruct((t, LANES), F32)],
        compiler_params=_cp(("parallel",)),
        name="wo",
    )(x, merged, w_o.astype(BF16), g_ffn.reshape(1, d).astype(F32), wr_hi, wr_lo, br)


def _router_kernel(lg_ref, idx_ref, gate_ref, rank_ref, cnt_ref, carry_ref, *, tm):
    i = pl.program_id(0)

    @pl.when(i == 0)
    def _():
        carry_ref[...] = jnp.zeros_like(carry_ref)

    l = lg_ref[...]
    lane = lax.broadcasted_iota(jnp.int32, l.shape, 1).astype(F32)
    vals, sels, idxs = [], [], []
    onehot = jnp.zeros(l.shape, F32)
    for _ in range(TOP_K):
        m = jnp.max(l, axis=-1, keepdims=True)
        idx = jnp.min(jnp.where(l == m, lane, float(LANES)), axis=-1, keepdims=True)
        sel = lane == idx
        vals.append(m)
        sels.append(sel)
        idxs.append(idx)
        onehot = onehot + jnp.where(sel, 1.0, 0.0)
        l = jnp.where(sel, -jnp.inf, l)
    es = [jnp.exp(vv - vals[0]) for vv in vals]
    den = es[0] + es[1] + es[2] + es[3]

    trow = lax.broadcasted_iota(jnp.int32, (tm, tm), 0)
    tcol = lax.broadcasted_iota(jnp.int32, (tm, tm), 1)
    tri = jnp.where(tcol < trow, 1.0, 0.0).astype(BF16)
    cum = _mm(tri, onehot.astype(BF16)) + carry_ref[...]
    carry = carry_ref[...] + jnp.sum(onehot, axis=0, keepdims=True)
    carry_ref[...] = carry
    cnt_ref[...] = carry

    idx_out = jnp.zeros(l.shape, F32)
    gate_out = jnp.zeros(l.shape, F32)
    rank_out = jnp.zeros(l.shape, F32)
    for kk in range(TOP_K):
        rk = jnp.sum(jnp.where(sels[kk], cum, 0.0), axis=-1, keepdims=True)
        idx_out = jnp.where(lane == kk, idxs[kk], idx_out)
        gate_out = jnp.where(lane == kk, es[kk] / den, gate_out)
        rank_out = jnp.where(lane == kk, rk, rank_out)
    idx_ref[...] = idx_out.T[0:SUBLANES, :].astype(jnp.int32)
    gate_ref[...] = gate_out
    rank_ref[...] = rank_out.T[0:SUBLANES, :].astype(jnp.int32)


def router(logits, tm):
    t = logits.shape[0]
    spec = pl.BlockSpec((tm, LANES), lambda i: (i, 0))
    kspec = pl.BlockSpec((SUBLANES, tm), lambda i: (0, i))
    return pl.pallas_call(
        functools.partial(_router_kernel, tm=tm),
        grid=(t // tm,),
        in_specs=[spec],
        out_specs=[kspec, spec, kspec, pl.BlockSpec((1, LANES), lambda i: (0, 0))],
        out_shape=[jax.ShapeDtypeStruct((SUBLANES, t), jnp.int32), jax.ShapeDtypeStruct((t, LANES), F32),
                   jax.ShapeDtypeStruct((SUBLANES, t), jnp.int32), jax.ShapeDtypeStruct((1, LANES), F32)],
        scratch_shapes=[pltpu.VMEM((1, LANES), F32)],
        compiler_params=_cp(("arbitrary",)),
        name="router",
    )(logits)


def _dispatch_kernel(dest_ref, zrow_ref, meta_ref, x_ref, rows_ref, zero_ref, zsem, sem, *, tm, ne, n_sub_total,
                     n_tok):
    i = pl.program_id(0)

    @pl.when(i == 0)
    def _():
        zero_ref[...] = jnp.zeros_like(zero_ref)

        def zcopy(row0):
            return pltpu.make_async_copy(
                zero_ref, rows_ref.at[pl.ds(pl.multiple_of(row0, SUB_ROWS), SUB_ROWS), :], zsem)

        for e in range(ne):
            zcopy(zrow_ref[e]).start()

        def tail(sb, carry):
            zcopy(sb * SUB_ROWS).start()
            return carry

        lax.fori_loop(meta_ref[0], n_sub_total, tail, 0)
        for e in range(ne):
            zcopy(zrow_ref[e]).wait()

        def tail_wait(sb, carry):
            zcopy(sb * SUB_ROWS).wait()
            return carry

        lax.fori_loop(meta_ref[0], n_sub_total, tail_wait, 0)

    base = i * tm

    def row_copy(r, dst_row):
        return pltpu.make_async_copy(x_ref.at[pl.ds(r, 1), :], rows_ref.at[pl.ds(dst_row, 1), :], sem)

    for r in range(tm):
        for kk in range(TOP_K):
            row_copy(r, dest_ref[base + (kk * n_tok + r)]).start(priority=kk % 2)
    for kk in range(TOP_K):
        pltpu.make_async_copy(x_ref, rows_ref.at[pl.ds(0, tm), :], sem).wait()


def dispatch(xn, dest_flat, zrow, used_sub, n_rows, tm):
    t, d = xn.shape
    ne = zrow.shape[0]
    grid_spec = pltpu.PrefetchScalarGridSpec(
        num_scalar_prefetch=3,
        grid=(t // tm,),
        in_specs=[pl.BlockSpec((tm, d), lambda i, *_: (i, 0))],
        out_specs=pl.BlockSpec(memory_space=pl.ANY),
        scratch_shapes=[pltpu.VMEM((SUB_ROWS, d), xn.dtype), pltpu.SemaphoreType.DMA, pltpu.SemaphoreType.DMA],
    )
    return pl.pallas_call(
        functools.partial(_dispatch_kernel, tm=tm, ne=ne, n_sub_total=n_rows // SUB_ROWS, n_tok=t),
        grid_spec=grid_spec,
        out_shape=jax.ShapeDtypeStruct((n_rows, d), xn.dtype),
        compiler_params=_cp(("arbitrary",)),
        name="dispatch",
    )(dest_flat, zrow, used_sub, xn)


def _expert_kernel(ge_ref, r0_ref, ns_ref, meta_ref, rows_ref, wg_ref, bg_ref, wu_ref, bu_ref, wd_ref, bd_ref,
                   out_ref, stage_ref, xb_ref, acc_ref, ost_ref, xsem, osem, *, n_sub_total):
    g = pl.program_id(0)
    j = pl.program_id(1)
    nj = pl.num_programs(1)
    ng = meta_ref[1]
    used = g < ng
    nsub = ns_ref[g]
    d = stage_ref.shape[1]

    def x_copy(grp, sidx):
        row0 = pl.multiple_of(r0_ref[grp] + sidx * SUB_ROWS, SUB_ROWS)
        return pltpu.make_async_copy(rows_ref.at[pl.ds(row0, SUB_ROWS), :],
                                     stage_ref.at[pl.ds(sidx * SUB_ROWS, SUB_ROWS), :], xsem)

    def o_copy(row0, sidx):
        return pltpu.make_async_copy(ost_ref.at[pl.ds(sidx * SUB_ROWS, SUB_ROWS), :],
                                     out_ref.at[pl.ds(pl.multiple_of(row0, SUB_ROWS), SUB_ROWS), :], osem)

    def for_subs(count, fn):
        for sidx in range(GROUP_SUBS):
            @pl.when(sidx < count)
            def _():
                fn(sidx)

    @pl.when((g == 0) & (j == 0))
    def _():
        for_subs(ns_ref[0], lambda sidx: x_copy(0, sidx).start())
        ost_ref[0:SUB_ROWS, :] = jnp.zeros((SUB_ROWS, d), F32)

        def tail(sb, carry):
            cp = o_copy(sb * SUB_ROWS, 0)
            cp.start()
            cp.wait()
            return carry

        lax.fori_loop(meta_ref[0], n_sub_total, tail, 0)

    @pl.when(used & (j == 0))
    def _():
        for_subs(nsub, lambda sidx: x_copy(g, sidx).wait())

        def stage_in(sidx):
            sl = pl.ds(sidx * SUB_ROWS, SUB_ROWS)
            xb_ref[sl, :] = stage_ref[sl, :].astype(BF16)
            acc_ref[sl, :] = jnp.zeros((SUB_ROWS, d), F32)

        for_subs(nsub, stage_in)

        @pl.when(g + 1 < ng)
        def _():
            for_subs(ns_ref[g + 1], lambda sidx: x_copy(g + 1, sidx).start())

    @pl.when(used)
    def _():
        for n in range(1, GROUP_SUBS + 1):
            @pl.when(nsub == n)
            def _():
                rows = n * SUB_ROWS
                xb = xb_ref[0:rows, :]
                hg = jnp.minimum(_mm(xb, wg_ref[...].astype(BF16)) + bg_ref[...], SWIGLU_LIMIT)
                hu = jnp.clip(_mm(xb, wu_ref[...].astype(BF16)) + bu_ref[...], -SWIGLU_LIMIT, SWIGLU_LIMIT)
                hh = (hu + 1.0) * hg * _sigmoid(SWIGLU_ALPHA * hg)
                acc_ref[0:rows, :] += _mm(hh.astype(BF16), wd_ref[...].astype(BF16))

    @pl.when(used & (j == nj - 1))
    def _():
        @pl.when(g > 0)
        def _():
            for_subs(ns_ref[g - 1], lambda sidx: o_copy(r0_ref[g - 1] + sidx * SUB_ROWS, sidx).wait())

        def result_out(sidx):
            sl = pl.ds(sidx * SUB_ROWS, SUB_ROWS)
            ost_ref[sl, :] = acc_ref[sl, :] + bd_ref[...]
            o_copy(r0_ref[g] + sidx * SUB_ROWS, sidx).start()

        for_subs(nsub, result_out)

        @pl.when(g == ng - 1)
        def _():
            for_subs(nsub, lambda sidx: o_copy(r0_ref[g] + sidx * SUB_ROWS, sidx).wait())


def experts(rows, g_expert, g_row0, g_nsub, meta, w_gate, b_gate, w_up, b_up, w_down, b_down, tn):
    n_rows, d = rows.shape
    ne, _, de = w_gate.shape
    n_groups_max = g_expert.shape[0]
    nj = de // tn
    gr = GROUP_SUBS * SUB_ROWS

    def eb(g, ge, meta):
        return ge[jnp.minimum(g, meta[1] - 1)]

    def jb(g, j, meta):
        return jnp.where(g < meta[1], j, nj - 1)

    grid_spec = pltpu.PrefetchScalarGridSpec(
        num_scalar_prefetch=4,
        grid=(n_groups_max, nj),
        in_specs=[
            pl.BlockSpec(memory_space=pl.ANY),
            pl.BlockSpec((None, d, tn), lambda g, j, ge, r0, ns, mt: (eb(g, ge, mt), 0, jb(g, j, mt))),
            pl.BlockSpec((None, 1, tn), lambda g, j, ge, r0, ns, mt: (eb(g, ge, mt), 0, jb(g, j, mt))),
            pl.BlockSpec((None, d, tn), lambda g, j, ge, r0, ns, mt: (eb(g, ge, mt), 0, jb(g, j, mt))),
            pl.BlockSpec((None, 1, tn), lambda g, j, ge, r0, ns, mt: (eb(g, ge, mt), 0, jb(g, j, mt))),
            pl.BlockSpec((None, tn, d), lambda g, j, ge, r0, ns, mt: (eb(g, ge, mt), jb(g, j, mt), 0)),
            pl.BlockSpec((None, 1, d), lambda g, j, ge, r0, ns, mt: (eb(g, ge, mt), 0, 0)),
        ],
        out_specs=pl.BlockSpec(memory_space=pl.ANY),
        scratch_shapes=[
            pltpu.VMEM((gr, d), F32),
            pltpu.VMEM((gr, d), BF16),
            pltpu.VMEM((gr, d), F32),
            pltpu.VMEM((gr, d), F32),
            pltpu.SemaphoreType.DMA,
            pltpu.SemaphoreType.DMA,
        ],
    )
    return pl.pallas_call(
        functools.partial(_expert_kernel, n_sub_total=n_rows // SUB_ROWS),
        grid_spec=grid_spec,
        out_shape=jax.ShapeDtypeStruct((n_rows, d), F32),
        compiler_params=_cp(("arbitrary", "arbitrary")),
        name="experts",
    )(g_expert, g_row0, g_nsub, meta, rows, w_gate, b_gate.reshape(ne, 1, de), w_up, b_up.reshape(ne, 1, de),
      w_down, b_down.reshape(ne, 1, d))


def _combine_kernel(dest_ref, rows_ref, h_ref, gate_ref, g_ref, o_ref, buf_ref, sem, *, tm, n_tok):
    i = pl.program_id(0)
    nt = pl.num_programs(0)
    n = tm * TOP_K
    slot = i % 2

    def issue_tile(tile, slt):
        base = tile * tm
        for r in range(tm):
            for kk in range(TOP_K):
                pltpu.make_async_copy(rows_ref.at[pl.ds(dest_ref[base + (kk * n_tok + r)], 1), :],
                                      buf_ref.at[slt, pl.ds(kk * tm + r, 1), :], sem.at[slt]
                                      ).start(priority=kk % 2)

    @pl.when(i == 0)
    def _():
        issue_tile(0, 0)

    for par in range(2):
        @pl.when(slot == par)
        def _():
            pltpu.make_async_copy(rows_ref.at[pl.ds(0, n), :], buf_ref.at[par], sem.at[par]).wait()

            @pl.when(i + 1 < nt)
            def _():
                issue_tile(i + 1, 1 - par)

    gates = gate_ref[...]
    h = h_ref[...]
    for kk in range(TOP_K):
        h = h + gates[:, kk:kk + 1] * buf_ref[slot, kk * tm:(kk + 1) * tm, :]
    ms = jnp.mean(h * h, axis=-1, keepdims=True)
    o_ref[...] = (h * lax.rsqrt(ms + EPS) * g_ref[...]).astype(o_ref.dtype)


def combine(dest_flat, out_rows, h1, gates, g_final, tm, out_dtype):
    t, d = h1.shape
    grid_spec = pltpu.PrefetchScalarGridSpec(
        num_scalar_prefetch=1,
        grid=(t // tm,),
        in_specs=[
            pl.BlockSpec(memory_space=pl.ANY),
            pl.BlockSpec((tm, d), lambda i, *_: (i, 0)),
            pl.BlockSpec((tm, LANES), lambda i, *_: (i, 0)),
            pl.BlockSpec((1, d), lambda i, *_: (0, 0)),
        ],
        out_specs=pl.BlockSpec((tm, d), lambda i, *_: (i, 0)),
        scratch_shapes=[pltpu.VMEM((2, tm * TOP_K, d), out_rows.dtype), pltpu.SemaphoreType.DMA((2,))],
    )
    return pl.pallas_call(
        functools.partial(_combine_kernel, tm=tm, n_tok=t),
        grid_spec=grid_spec,
        out_shape=jax.ShapeDtypeStruct((t, d), out_dtype),
        compiler_params=_cp(("arbitrary",)),
        name="combine",
    )(dest_flat, out_rows, h1, gates, g_final.reshape(1, d).astype(F32))


def _tile(n, pref):
    while n % pref:
        pref //= 2
    return pref


def _group_tables(counts, n_groups_max):
    ne = counts.shape[0]
    nsub_e = (counts + SUB_ROWS - 1) // SUB_ROWS
    pend_sub = jnp.cumsum(nsub_e)
    pstart = (pend_sub - nsub_e) * SUB_ROWS
    ng_e = (nsub_e + GROUP_SUBS - 1) // GROUP_SUBS
    base_e = nsub_e // jnp.maximum(ng_e, 1)
    rem_e = nsub_e - base_e * ng_e
    cum_ng = jnp.cumsum(ng_e)
    n_groups = cum_ng[-1]
    g = jnp.arange(n_groups_max, dtype=jnp.int32)
    e_g = jnp.minimum(jnp.sum((g[:, None] >= cum_ng[None, :]).astype(jnp.int32), axis=1), ne - 1)
    lg = g - (cum_ng[e_g] - ng_e[e_g])
    valid = g < n_groups
    g_nsub = jnp.where(valid, base_e[e_g] + (lg < rem_e[e_g]).astype(jnp.int32), 0)
    g_row0 = jnp.where(valid, pstart[e_g] + SUB_ROWS * (lg * base_e[e_g] + jnp.minimum(lg, rem_e[e_g])), 0)
    zrow = pstart + jnp.maximum(nsub_e - 1, 0) * SUB_ROWS
    meta = jnp.stack([pend_sub[-1], n_groups]).astype(jnp.int32)
    i32 = lambda z: z.astype(jnp.int32)
    return i32(pstart), i32(zrow), i32(e_g), i32(g_row0), i32(g_nsub), meta


def kernel(x, mem, g_mix, w_in, mu_rwkv, rwkv_w0, rwkv_w2, rwkv_a0, rwkv_a2, rwkv_g2, rwkv_k_k, rwkv_k_a,
           rwkv_r_k, rwkv_ln_w, rwkv_ln_b, w_rwkv_out, lru_conv_w, lru_conv_b, lru_w_r, lru_b_r, lru_w_i,
           lru_b_i, lru_lambda, w_lru_out, g_mem, w_mem_k, w_mem_v, w_xa_out, w_o, g_ffn, w_router,
           b_router, w_gate, b_gate, w_up, b_up, w_down, b_down, g_final):
    bsz, seq, d = x.shape
    assert w_in.shape[0] == 1, "single-layer block"
    l = 0
    t = bsz * seq
    c = rwkv_w0.shape[1]
    n_lora = LORA_W + LORA_A + LORA_G
    ne = w_router.shape[2]
    assert lru_conv_b.shape[1] == c and w_mem_k.shape[2] == c and c % LORA_PAD == 0

    h = x.reshape(t, d).astype(F32)
    memf = mem.reshape(bsz * mem.shape[1], d)
    wi = w_in[l].astype(BF16)
    c0 = 3 * c + n_lora
    w_a = jnp.concatenate(
        [wi[:, 0:3 * c], wi[:, c0:c0 + 2 * c], wi[:, 3 * c:c0], jnp.zeros((d, LORA_PAD - n_lora), BF16)], axis=1)
    w_b = wi[:, c0 + 2 * c:]

    xn = norm_cast(h, g_mix[l], _tile(t, 512))
    p_a = matmul(xn, w_a, _tile(t, 1024), _tile(w_a.shape[1], 512), F32)
    p_b = matmul(xn, w_b, _tile(t, 1024), _tile(w_b.shape[1], 1024), BF16)
    mn = norm_cast(memf, g_mem[l], _tile(memf.shape[0], 256))
    kv = matmul(mn, jnp.concatenate([w_mem_k[l], w_mem_v[l]], axis=1).astype(BF16),
                _tile(memf.shape[0], 256), _tile(2 * c, 512), BF16)

    y_a = rwkv_branch(p_a, bsz, seq, mu_rwkv[l], rwkv_w0[l], rwkv_w2[l], rwkv_a0[l], rwkv_a2[l],
                      rwkv_g2[l], rwkv_k_k[l], rwkv_k_a[l], rwkv_r_k[l].reshape(-1), rwkv_ln_w[l],
                      rwkv_ln_b[l], _tile(seq, 256), 5 * c)
    y_b = lru_branch(p_a, bsz, seq, 3 * c, 4 * c, lru_conv_w[l], lru_conv_b[l], lru_w_r[l], lru_b_r[l],
                     lru_w_i[l], lru_b_i[l], lru_lambda[l], _tile(seq, 256))
    y_c = xattn_branch(p_b, kv, bsz, seq, _tile(seq, 512))
    merged = merge(y_a, y_b, y_c, w_rwkv_out[l].astype(BF16), w_lru_out[l].astype(BF16),
                   w_xa_out[l].astype(BF16), p_b, c, _tile(t, 512), _tile(d, 1024))
    h1, xn2, logits = wo_block(h, merged, w_o[l], g_ffn[l], w_router[l], b_router[l], _tile(t, 256))

    idx, gates, rank, counts = router(logits, _tile(t, 512))
    counts = counts[0, :ne].astype(jnp.int32)
    n_rows = t * TOP_K + ne * SUB_ROWS
    n_groups_max = (t * TOP_K) // (GROUP_SUBS * SUB_ROWS) + ne
    pstart, zrow, g_expert, g_row0, g_nsub, meta = _group_tables(counts, n_groups_max)
    dest = (pstart[idx[:TOP_K]] + rank[:TOP_K]).reshape(-1)

    tmd = _tile(t, 256)
    rows = dispatch(xn2, dest, zrow, meta, n_rows, tmd)
    out_rows = experts(rows, g_expert, g_row0, g_nsub, meta, w_gate[l], b_gate[l], w_up[l], b_up[l],
                       w_down[l], b_down[l], _tile(w_gate.shape[3], 256))
    out = combine(dest, out_rows, h1, gates, g_final, tmd, x.dtype)
    return out.reshape(bsz, seq, d)
```

```python
import functools

import jax
import jax.numpy as jnp
from jax import lax
from jax.experimental import pallas as pl
from jax.experimental.pallas import tpu as pltpu

F32 = jnp.float32
BF16 = jnp.bfloat16

EPS = 1e-6
GN_EPS = 64e-5
RG_C = 8.0
SWIGLU_LIMIT = 7.0
SWIGLU_ALPHA = 1.702
RWKV_HEAD = 64
LORA_W = 64
LORA_A = 64
LORA_G = 160
CONV_W = 4
LRU_BLOCKS = 16
XA_HEADS = 4
TOP_K = 4
N_BRANCH = 3

LANES = 128
SUBLANES = 8
CHUNK = 64
SUB = 16
SUB_ROWS = 256
GROUP_SUBS = 4
LORA_PAD = 512
NEG_BIG = -1e30
VMEM_LIMIT = 56 * 1024 * 1024


def _cp(sem, vmem=None):
    return pltpu.CompilerParams(dimension_semantics=sem, vmem_limit_bytes=vmem or VMEM_LIMIT)


def _mm(a, b):
    return jnp.dot(a, b, preferred_element_type=F32)


def _mm_nt(a, b):
    return lax.dot_general(a, b, (((1,), (1,)), ((), ())), preferred_element_type=F32)


def _mm_tn(a, b):
    return lax.dot_general(a, b, (((0,), (0,)), ((), ())), preferred_element_type=F32)


def _split(x):
    hi = x.astype(BF16)
    lo = (x - hi.astype(F32)).astype(BF16)
    return hi, lo


def _mm_exact_rhs(x, m):
    hi, lo = _split(x)
    return _mm(hi, m) + _mm(lo, m)


def _mm_exact_lhs(m, x):
    hi, lo = _split(x)
    return _mm(m, hi) + _mm(m, lo)


def _sigmoid(x):
    return 1.0 / (1.0 + jnp.exp(-x))


def _softplus(x):
    return jnp.maximum(x, 0.0) + jnp.log(1.0 + jnp.exp(-jnp.abs(x)))


def _norm_kernel(x_ref, g_ref, o_ref):
    x = x_ref[...].astype(F32)
    ms = jnp.mean(x * x, axis=-1, keepdims=True)
    o_ref[...] = (x * lax.rsqrt(ms + EPS) * g_ref[...]).astype(o_ref.dtype)


def norm_cast(x, g, tm, out_dtype=BF16):
    t, d = x.shape
    return pl.pallas_call(
        _norm_kernel,
        grid=(t // tm,),
        in_specs=[pl.BlockSpec((tm, d), lambda i: (i, 0)), pl.BlockSpec((1, d), lambda i: (0, 0))],
        out_specs=pl.BlockSpec((tm, d), lambda i: (i, 0)),
        out_shape=jax.ShapeDtypeStruct((t, d), out_dtype),
        compiler_params=_cp(("parallel",)),
        name="norm_cast",
    )(x, g.reshape(1, d).astype(F32))


def _matmul_kernel(a_ref, w_ref, o_ref):
    o_ref[...] = _mm(a_ref[...], w_ref[...]).astype(o_ref.dtype)


def matmul(a, w, tm, tn, out_dtype):
    m, k = a.shape
    n = w.shape[1]
    return pl.pallas_call(
        _matmul_kernel,
        grid=(m // tm, n // tn),
        in_specs=[pl.BlockSpec((tm, k), lambda i, j: (i, 0)), pl.BlockSpec((k, tn), lambda i, j: (0, j))],
        out_specs=pl.BlockSpec((tm, tn), lambda i, j: (i, j)),
        out_shape=jax.ShapeDtypeStruct((m, n), out_dtype),
        compiler_params=_cp(("parallel", "parallel")),
        name="matmul",
    )(a, w)


def _rwkv_kernel(r_ref, k_ref, v_ref, lo_ref, rp_ref, kp_ref, vp_ref, lop_ref,
                 mur_ref, muk_ref, muv_ref, mulo_ref, w0_ref, a0_ref, kk_ref, ka_ref, rk_ref,
                 lnw_ref, lnb_ref, w2_ref, a2_ref, g2_ref, o_ref,
                 s_ref, rpq_ref, y0q_ref, abq_ref, gtq_ref, peq_ref, gq_ref, bnq_ref,
                 ra_ref, va_ref, k2a_ref, ava_ref, bva_ref, lda_ref, ca_ref, ga_ref, bna_ref, *, tm, nb):
    i = pl.program_id(1)
    first = i == 0
    a_refs = (ra_ref, va_ref, k2a_ref, ava_ref, bva_ref, lda_ref, ca_ref, ga_ref, bna_ref)

    @pl.when(first)
    def _():
        for ref in (s_ref, rpq_ref, y0q_ref, abq_ref, gtq_ref, peq_ref, gq_ref, bnq_ref) + a_refs:
            ref[...] = jnp.zeros_like(ref)

    bf = lambda z: z.astype(BF16)
    bs = range(nb)
    n2 = 2 * CHUNK
    nch = tm // CHUNK
    items = [(b, q) for q in range(nch) for b in bs]
    rng = range(len(items))
    inv_n = 1.0 / RWKV_HEAD

    lane = lax.broadcasted_iota(jnp.int32, (LANES, LANES), 1)
    sub = lax.broadcasted_iota(jnp.int32, (LANES, LANES), 0)
    e2 = jnp.where((lane // RWKV_HEAD) == (sub // RWKV_HEAD), 1.0, 0.0).astype(BF16)

    def mix(x_ref, p_ref, mu_ref, b):
        x = x_ref[b]
        prev = jnp.where(first, 0.0, p_ref[b, SUBLANES - 1:SUBLANES, :])
        row = lax.broadcasted_iota(jnp.int32, x.shape, 0)
        sh = jnp.where(row == 0, prev, pltpu.roll(x, 1, axis=0))
        return x + (sh - x) * mu_ref[...]

    def recurrence():
        s = [s_ref[b] for b in bs]
        ys = [[] for _ in bs]
        for n, (b, q) in enumerate(items):
            sb = bf(s[b])
            yst = _mm_nt(rpq_ref[n], sb) + y0q_ref[n]
            s[b] = s[b] * peq_ref[n, 0:1, :] + _mm_nt(sb, abq_ref[n]) + gtq_ref[n]
            ys[b].append(yst[0:CHUNK] + yst[CHUNK:n2])
            yield
        for b in bs:
            s_ref[b] = s[b]
        y = [jnp.concatenate(z, axis=0) if len(z) > 1 else z[0] for z in ys]
        mean = [_mm_exact_rhs(z, e2) * inv_n for z in y]
        yield
        yc = [y[b] - mean[b] for b in bs]
        var = [_mm_exact_rhs(z * z, e2) * inv_n for z in yc]
        yield
        for b in bs:
            yn = yc[b] * lax.rsqrt(var[b] + GN_EPS) * lnw_ref[...] + lnb_ref[...]
            o_ref[b] = ((yn + bnq_ref[b]) * gq_ref[b]).astype(o_ref.dtype)

    def prologue():
        r = [mix(r_ref, rp_ref, mur_ref, b) for b in bs]
        yield
        k = [mix(k_ref, kp_ref, muk_ref, b) for b in bs]
        yield
        v = [mix(v_ref, vp_ref, muv_ref, b) for b in bs]
        yield
        lo = [mix(lo_ref, lop_ref, mulo_ref, b) for b in bs]
        yield
        wl = [_mm(bf(jnp.tanh(z[:, 0:LANES])), w2_ref[...]) for z in lo]
        al = [_mm(bf(z[:, 0:LANES]), a2_ref[...]) for z in lo]
        yield
        g = [_mm(bf(_sigmoid(z[:, LANES:3 * LANES])), g2_ref[...]) for z in lo]
        yield
        ld = [-jnp.exp(-_softplus(-(w0_ref[...] + z)) - 0.5) for z in wl]
        a_sig = [_sigmoid(a0_ref[...] + z) for z in al]
        yield
        kk = [z * kk_ref[...] for z in k]
        kk = [z * lax.rsqrt(jnp.maximum(_mm_exact_rhs(z * z, e2), 1e-24)) for z in kk]
        yield
        k2 = [k[b] * (1.0 + (a_sig[b] - 1.0) * ka_ref[...]) for b in bs]
        bonus = [_mm_exact_rhs(r[b] * k2[b] * rk_ref[...], e2) * v[b] for b in bs]
        yield
        trow = lax.broadcasted_iota(jnp.int32, (tm, tm), 0)
        tcol = lax.broadcasted_iota(jnp.int32, (tm, tm), 1)
        tri = jnp.where(((trow // CHUNK) == (tcol // CHUNK)) & (tcol <= trow), 1.0, 0.0).astype(BF16)
        c = [_mm_exact_lhs(tri, z) for z in ld]
        yield
        for b in bs:
            vals = (r[b], v[b], k2[b], -kk[b], kk[b] * a_sig[b], ld[b], c[b], g[b], bonus[b])
            for ref, val in zip(a_refs, vals):
                ref[b] = val

    r, v, k2, a_vec, b_vec, ld, c, g, bonus = ([ref[b] for b in bs] for ref in a_refs)

    lane_row = lax.broadcasted_iota(jnp.int32, (CHUNK, LANES), 1)
    head_a = lane_row < RWKV_HEAD

    def stack(x):
        return jnp.concatenate([jnp.where(head_a, x, 0.0), jnp.where(head_a, 0.0, x)], axis=0)

    ri = lax.broadcasted_iota(jnp.int32, (n2, n2), 0)
    ci = lax.broadcasted_iota(jnp.int32, (n2, n2), 1)
    strict = ri > ci
    incl = ri >= ci
    blk = (ri // SUB) == (ci // SUB)
    eye = jnp.where(ri == ci, 1.0, 0.0)

    sl = lambda q: slice(q * CHUNK, (q + 1) * CHUNK)
    cs = [c[b][sl(q)] for b, q in items]
    lds = [ld[b][sl(q)] for b, q in items]
    c_ends = [z[CHUNK - 1:CHUNK, :] for z in cs]
    e_pos = [jnp.exp(z) for z in cs]
    e_neg = [jnp.exp(-z) for z in cs]
    e_end = [jnp.exp(ce - z) for ce, z in zip(c_ends, cs)]
    p_end = [jnp.exp(ce) for ce in c_ends]
    rt32 = [stack(r[b][sl(q)] * e_pos[n]) for n, (b, q) in enumerate(items)]
    at = [bf(stack(a_vec[b][sl(q)] * jnp.exp(cs[n] - lds[n]))) for n, (b, q) in enumerate(items)]
    rt = [bf(z) for z in rt32]
    bt = [bf(stack(b_vec[b][sl(q)] * e_neg[n])) for n, (b, q) in enumerate(items)]
    kt = [bf(stack(k2[b][sl(q)] * e_neg[n])) for n, (b, q) in enumerate(items)]
    bh = [bf(stack(b_vec[b][sl(q)] * e_end[n])) for n, (b, q) in enumerate(items)]
    kh = [bf(stack(k2[b][sl(q)] * e_end[n])) for n, (b, q) in enumerate(items)]
    vs = [bf(stack(v[b][sl(q)])) for b, q in items]

    ops = {}

    def chunk_operators():
        gram = [_mm_nt(jnp.concatenate([at[n], rt[n]], axis=0), jnp.concatenate([bt[n], kt[n]], axis=0))
                for n in rng]
        yield
        l_ab = [jnp.where(strict, z[0:n2, 0:n2], 0.0) for z in gram]
        l_ak = [bf(jnp.where(strict, z[0:n2, n2:], 0.0)) for z in gram]
        m_rb = [bf(jnp.where(incl, z[n2:, 0:n2], 0.0)) for z in gram]
        m_rk = [bf(jnp.where(incl, z[n2:, n2:], 0.0)) for z in gram]
        yield
        d = [jnp.where(blk, z, 0.0) for z in l_ab]
        l_off = [bf(l_ab[n] - d[n]) for n in rng]
        db = [bf(z) for z in d]
        p = [eye + z for z in d]
        yield
        d2 = [bf(_mm(z, z)) for z in db]
        yield
        lv = [_mm(l_ak[n], vs[n]) for n in rng]
        yield
        p = [p[n] + _mm(d2[n], bf(p[n])) for n in rng]
        yield
        d4 = [bf(_mm(z, z)) for z in d2]
        yield
        p = [p[n] + _mm(d4[n], bf(p[n])) for n in rng]
        yield
        d8 = [bf(_mm(z, z)) for z in d4]
        yield
        td = [bf(p[n] + _mm(d8[n], bf(p[n]))) for n in rng]
        yield
        mm = [bf(_mm(td[n], l_off[n])) for n in rng]
        yield
        x = [_mm(td[n], jnp.concatenate([at[n], bf(lv[n])], axis=1)) for n in rng]
        yield
        m2 = [bf(_mm(z, z)) for z in mm]
        yield
        x = [x[n] + _mm(mm[n], bf(x[n])) for n in rng]
        yield
        xb = [bf(x[n] + _mm(m2[n], bf(x[n]))) for n in rng]
        wb = [z[:, 0:n2] for z in xb]
        u0b = [z[:, n2:] for z in xb]
        yield
        t1 = [_mm(m_rb[n], xb[n]) for n in rng]
        yield
        rp = [bf(rt32[n] + t1[n][:, 0:n2]) for n in rng]
        y0 = [t1[n][:, n2:] + _mm(m_rk[n], vs[n]) for n in rng]
        yield
        ab = [bf(_mm_tn(bh[n], wb[n])) for n in rng]
        yield
        gt = [_mm_tn(u0b[n], bh[n]) + _mm_tn(vs[n], kh[n]) for n in rng]
        ops.update(rp=rp, y0=y0, ab=ab, gt=gt)

    gens = [[chunk_operators(), 19, 0], [prologue(), 10, 0], [recurrence(), len(items) + 2, 0]]
    while gens:
        entry = min(gens, key=lambda e: e[2] / e[1])
        entry[2] += 1
        if next(entry[0], "done") == "done":
            gens.remove(entry)

    for n in rng:
        rpq_ref[n] = ops["rp"][n]
        y0q_ref[n] = ops["y0"][n]
        abq_ref[n] = ops["ab"][n]
        gtq_ref[n] = ops["gt"][n]
        peq_ref[n] = jnp.broadcast_to(p_end[n], (SUBLANES, LANES))
    for b in bs:
        gq_ref[b] = g[b]
        bnq_ref[b] = bonus[b]


def rwkv_branch(p_a, bsz, seq, mu, w0, w2, a0, a2, g2, k_k, k_a, r_k, ln_w, ln_b, tm, col0_lora):
    t, na = p_a.shape
    c = w0.shape[0]
    nj = c // LANES
    nt = seq // tm
    lw = LORA_PAD
    lora_blk = col0_lora // lw
    assert col0_lora % lw == 0 and seq % tm == 0 and tm % CHUNK == 0
    p3 = p_a.reshape(bsz, seq, na)

    mu_rkv = mu[:3 * c].reshape(1, 3 * c)
    mu_lo = jnp.pad(mu[3 * c:], (0, lw - (LORA_W + LORA_A + LORA_G))).reshape(1, lw)
    w2p = jnp.pad(w2, ((0, LANES - LORA_W), (0, 0))).astype(BF16)
    a2p = jnp.pad(a2, ((LORA_W, LANES - LORA_W - LORA_A), (0, 0))).astype(BF16)
    g2p = jnp.pad(g2, ((0, 2 * LANES - LORA_G), (0, 0))).astype(BF16)
    vec = lambda z: z.reshape(1, c).astype(F32)

    rpb = tm // SUBLANES

    tile = lambda i: jnp.minimum(i, nt - 1)
    prow = lambda i: jnp.maximum(tile(i) * rpb - 1, 0)

    def cur(off):
        return pl.BlockSpec((bsz, tm, LANES), lambda j, i: (0, tile(i), off * nj + j))

    def prev(off):
        return pl.BlockSpec((bsz, SUBLANES, LANES), lambda j, i: (0, prow(i), off * nj + j))

    def slab(rows):
        return pl.BlockSpec((rows, LANES), lambda j, i: (0, j))

    n_items = bsz * (tm // CHUNK)
    n2 = 2 * CHUNK
    in_specs = [
        cur(0), cur(1), cur(2),
        pl.BlockSpec((bsz, tm, lw), lambda j, i: (0, tile(i), lora_blk)),
        prev(0), prev(1), prev(2),
        pl.BlockSpec((bsz, SUBLANES, lw), lambda j, i: (0, prow(i), lora_blk)),
        pl.BlockSpec((1, LANES), lambda j, i: (0, j)),
        pl.BlockSpec((1, LANES), lambda j, i: (0, nj + j)),
        pl.BlockSpec((1, LANES), lambda j, i: (0, 2 * nj + j)),
        pl.BlockSpec((1, lw), lambda j, i: (0, 0)),
        slab(1), slab(1), slab(1), slab(1), slab(1), slab(1), slab(1),
        slab(LANES), slab(LANES), slab(2 * LANES),
    ]
    out = pl.pallas_call(
        functools.partial(_rwkv_kernel, tm=tm, nb=bsz),
        grid=(nj, nt + 2),
        in_specs=in_specs,
        out_specs=pl.BlockSpec((bsz, tm, LANES), lambda j, i: (0, jnp.maximum(i - 2, 0), j)),
        out_shape=jax.ShapeDtypeStruct((bsz, seq, c), BF16),
        scratch_shapes=[
            pltpu.VMEM((bsz, n2, n2), F32),
            pltpu.VMEM((n_items, n2, n2), BF16),
            pltpu.VMEM((n_items, n2, n2), F32),
            pltpu.VMEM((n_items, n2, n2), BF16),
            pltpu.VMEM((n_items, n2, n2), F32),
            pltpu.VMEM((n_items, SUBLANES, LANES), F32),
            pltpu.VMEM((bsz, tm, LANES), F32),
            pltpu.VMEM((bsz, tm, LANES), F32),
        ] + [pltpu.VMEM((bsz, tm, LANES), F32)] * 9,
        compiler_params=_cp(("parallel", "arbitrary")),
        name="rwkv",
    )(p3, p3, p3, p3, p3, p3, p3, p3,
      mu_rkv, mu_rkv, mu_rkv, mu_lo, vec(w0), vec(a0), vec(k_k), vec(k_a), vec(r_k), vec(ln_w), vec(ln_b),
      w2p, a2p, g2p)
    return out.reshape(t, c)


def _lru_kernel(x_ref, xp_ref, y_ref, cw_ref, cb_ref, wr_ref, br_ref, wi_ref, bi_ref, sp_ref,
                o_ref, h_ref, *, tm, nq):
    i = pl.program_id(1)
    first = i == 0

    @pl.when(first)
    def _():
        h_ref[...] = jnp.zeros_like(h_ref)

    x = x_ref[...]
    prev = jnp.where(first, 0.0, xp_ref[...])
    xx = jnp.concatenate([prev, x], axis=0)
    xc = cb_ref[...] + x * cw_ref[0:1, :]
    for j in range(1, CONV_W):
        xc = xc + xx[SUBLANES - j:SUBLANES - j + tm, :] * cw_ref[j:j + 1, :]

    xcb = xc.astype(BF16)
    qw = x.shape[1] // nq
    gr = jnp.concatenate([_mm(xcb[:, q * qw:(q + 1) * qw], wr_ref[q]) for q in range(nq)], axis=1)
    gi = jnp.concatenate([_mm(xcb[:, q * qw:(q + 1) * qw], wi_ref[q]) for q in range(nq)], axis=1)
    gate_r = _sigmoid(gr + br_ref[...])
    gate_i = _sigmoid(gi + bi_ref[...])
    log_a = -RG_C * gate_r * sp_ref[...]
    a = jnp.exp(log_a)
    mult = jnp.sqrt(jnp.maximum(1.0 - jnp.exp(2.0 * log_a), 0.0))
    row = lax.broadcasted_iota(jnp.int32, x.shape, 0)
    mult = jnp.where(first & (row == 0), 1.0, mult)
    b = mult * gate_i * xc

    rin = row % SUBLANES
    sft = 1
    while sft < SUBLANES:
        keep = rin >= sft
        a_sh = jnp.where(keep, pltpu.roll(a, sft, axis=0), 1.0)
        b_sh = jnp.where(keep, pltpu.roll(b, sft, axis=0), 0.0)
        b = b + a * b_sh
        a = a * a_sh
        sft *= 2
    carry = h_ref[...]
    hs = []
    for gi in range(tm // SUBLANES):
        rows = slice(gi * SUBLANES, (gi + 1) * SUBLANES)
        hg = a[rows] * carry + b[rows]
        carry = hg[SUBLANES - 1:SUBLANES, :]
        hs.append(hg)
    h = jnp.concatenate(hs, axis=0)
    h_ref[...] = carry

    py = y_ref[...]
    gelu = 0.5 * py * (1.0 + jnp.tanh(0.7978845608028654 * (py + 0.044715 * py * py * py)))
    o_ref[...] = (h * gelu).astype(o_ref.dtype)


def lru_branch(p_a, bsz, seq, col_x, col_y, conv_w, conv_b, w_r, b_r, w_i, b_i, lam, tm):
    t = p_a.shape[0]
    c = conv_b.shape[0]
    nt = seq // tm
    rpb = tm // SUBLANES
    bx, by = col_x // c, col_y // c
    assert col_x % c == 0 and col_y % c == 0
    nblk, bw = w_r.shape[0], w_r.shape[1]
    per = max(1, (2 * LANES) // bw)
    nq = nblk // per

    def blockdiag(w):
        w = w.reshape(nq, per, bw, bw)
        eye = jnp.eye(per, dtype=w.dtype)
        return jnp.einsum('qpab,pr->qparb', w, eye).reshape(nq, per * bw, per * bw).astype(BF16)

    vec = lambda z: z.reshape(1, c).astype(F32)
    sp = jax.nn.softplus(-lam)
    return pl.pallas_call(
        functools.partial(_lru_kernel, tm=tm, nq=nq),
        grid=(bsz, nt),
        in_specs=[
            pl.BlockSpec((tm, c), lambda b, i: (b * nt + i, bx)),
            pl.BlockSpec((SUBLANES, c), lambda b, i: (jnp.maximum((b * nt + i) * rpb - 1, 0), bx)),
            pl.BlockSpec((tm, c), lambda b, i: (b * nt + i, by)),
            pl.BlockSpec((CONV_W, c), lambda b, i: (0, 0)),
            pl.BlockSpec((1, c), lambda b, i: (0, 0)),
            pl.BlockSpec((nq, per * bw, per * bw), lambda b, i: (0, 0, 0)),
            pl.BlockSpec((1, c), lambda b, i: (0, 0)),
            pl.BlockSpec((nq, per * bw, per * bw), lambda b, i: (0, 0, 0)),
            pl.BlockSpec((1, c), lambda b, i: (0, 0)),
            pl.BlockSpec((1, c), lambda b, i: (0, 0)),
        ],
        out_specs=pl.BlockSpec((tm, c), lambda b, i: (b * nt + i, 0)),
        out_shape=jax.ShapeDtypeStruct((t, c), BF16),
        scratch_shapes=[pltpu.VMEM((1, c), F32)],
        compiler_params=_cp(("parallel", "arbitrary")),
        name="lru",
    )(p_a, p_a, p_a, conv_w.astype(F32), vec(conv_b), blockdiag(w_r), vec(b_r), blockdiag(w_i), vec(b_i),
      vec(sp))


def _xattn_kernel(q_ref, k_ref, v_ref, o_ref, *, hd):
    scale = hd ** -0.5
    outs = []
    for h in range(XA_HEADS):
        sl = slice(h * hd, (h + 1) * hd)
        s = _mm_nt(q_ref[:, sl], k_ref[:, sl]) * scale
        m = jnp.max(s, axis=-1, keepdims=True)
        e = jnp.exp(s - m)
        pr = e / jnp.sum(e, axis=-1, keepdims=True)
        outs.append(_mm(pr.astype(BF16), v_ref[:, sl]))
    o_ref[...] = jnp.concatenate(outs, axis=1).astype(o_ref.dtype)


def xattn_branch(p_b, kv, bsz, seq, tm):
    t = p_b.shape[0]
    c = kv.shape[1] // 2
    mlen = kv.shape[0] // bsz
    nt = seq // tm
    return pl.pallas_call(
        functools.partial(_xattn_kernel, hd=c // XA_HEADS),
        grid=(bsz, nt),
        in_specs=[
            pl.BlockSpec((tm, c), lambda b, i: (b * nt + i, 0)),
            pl.BlockSpec((mlen, c), lambda b, i: (b, 0)),
            pl.BlockSpec((mlen, c), lambda b, i: (b, 1)),
        ],
        out_specs=pl.BlockSpec((tm, c), lambda b, i: (b * nt + i, 0)),
        out_shape=jax.ShapeDtypeStruct((t, c), BF16),
        compiler_params=_cp(("parallel", "parallel")),
        name="xattn",
    )(p_b, kv, kv)


def _merge_kernel(ya_ref, yb_ref, yc_ref, wa_ref, wb_ref, wc_ref, ga_ref, gb_ref, gc_ref, o_ref):
    acc = _sigmoid(ga_ref[...].astype(F32)) * _mm(ya_ref[...], wa_ref[...])
    acc = acc + _sigmoid(gb_ref[...].astype(F32)) * _mm(yb_ref[...], wb_ref[...])
    acc = acc + _sigmoid(gc_ref[...].astype(F32)) * _mm(yc_ref[...], wc_ref[...])
    o_ref[...] = acc.astype(o_ref.dtype)


def merge(y_a, y_b, y_c, w_a, w_b, w_c, p_b, gate_col0, tm, tn):
    t, c = y_a.shape
    d = w_a.shape[1]
    g0 = gate_col0 // tn
    gstep = d // tn
    assert gate_col0 % tn == 0
    yspec = pl.BlockSpec((tm, c), lambda i, j: (i, 0))
    wspec = pl.BlockSpec((c, tn), lambda i, j: (0, j))
    gspec = lambda n: pl.BlockSpec((tm, tn), lambda i, j: (i, g0 + n * gstep + j))
    return pl.pallas_call(
        _merge_kernel,
        grid=(t // tm, d // tn),
        in_specs=[yspec, yspec, yspec, wspec, wspec, wspec, gspec(0), gspec(1), gspec(2)],
        out_specs=pl.BlockSpec((tm, tn), lambda i, j: (i, j)),
        out_shape=jax.ShapeDtypeStruct((t, d), BF16),
        compiler_params=_cp(("parallel", "parallel")),
        name="merge",
    )(y_a, y_b, y_c, w_a, w_b, w_c, p_b, p_b, p_b)


def _wo_kernel(x_ref, m_ref, wo_ref, g_ref, wrh_ref, wrl_ref, br_ref, h_ref, xn_ref, lg_ref):
    h = x_ref[...].astype(F32) + _mm(m_ref[...], wo_ref[...])
    h_ref[...] = h
    ms = jnp.mean(h * h, axis=-1, keepdims=True)
    xn = h * lax.rsqrt(ms + EPS) * g_ref[...]
    xn_ref[...] = xn
    hi, lo = _split(xn)
    lg_ref[...] = _mm(hi, wrh_ref[...]) + _mm(lo, wrh_ref[...]) + _mm(hi, wrl_ref[...]) + br_ref[...]


def wo_block(x, merged, w_o, g_ffn, w_router, b_router, tm):
    t, d = x.shape
    ne = w_router.shape[1]
    wr = jnp.pad(w_router.astype(F32), ((0, 0), (0, LANES - ne)))
    wr_hi = wr.astype(BF16)
    wr_lo = (wr - wr_hi.astype(F32)).astype(BF16)
    br = jnp.pad(b_router.astype(F32), (0, LANES - ne), constant_values=NEG_BIG).reshape(1, LANES)
    row = pl.BlockSpec((tm, d), lambda i: (i, 0))
    full = lambda r, c: pl.BlockSpec((r, c), lambda i: (0, 0))
    return pl.pallas_call(
        _wo_kernel,
        grid=(t // tm,),
        in_specs=[row, row, full(d, d), full(1, d), full(d, LANES), full(d, LANES), full(1, LANES)],
        out_specs=[row, row, pl.BlockSpec((tm, LANES), lambda i: (i, 0))],
        out_shape=[jax.ShapeDtypeStruct((t, d), F32), jax.ShapeDtypeStruct((t, d), F32),
                   jax.ShapeDtypeStruct((t, LANES), F32)],
        compiler_params=_cp(("parallel",)),
        name="wo",
    )(x, merged, w_o.astype(BF16), g_ffn.reshape(1, d).astype(F32), wr_hi, wr_lo, br)


def _router_kernel(lg_ref, idx_ref, gate_ref, rank_ref, cnt_ref, carry_ref, *, tm):
    i = pl.program_id(0)

    @pl.when(i == 0)
    def _():
        carry_ref[...] = jnp.zeros_like(carry_ref)

    l = lg_ref[...]
    lane = lax.broadcasted_iota(jnp.int32, l.shape, 1).astype(F32)
    vals, sels, idxs = [], [], []
    onehot = jnp.zeros(l.shape, F32)
    for _ in range(TOP_K):
        m = jnp.max(l, axis=-1, keepdims=True)
        idx = jnp.min(jnp.where(l == m, lane, float(LANES)), axis=-1, keepdims=True)
        sel = lane == idx
        vals.append(m)
        sels.append(sel)
        idxs.append(idx)
        onehot = onehot + jnp.where(sel, 1.0, 0.0)
        l = jnp.where(sel, -jnp.inf, l)
    es = [jnp.exp(vv - vals[0]) for vv in vals]
    den = es[0] + es[1] + es[2] + es[3]

    trow = lax.broadcasted_iota(jnp.int32, (tm, tm), 0)
    tcol = lax.broadcasted_iota(jnp.int32, (tm, tm), 1)
    tri = jnp.where(tcol < trow, 1.0, 0.0).astype(BF16)
    cum = _mm(tri, onehot.astype(BF16)) + carry_ref[...]
    carry = carry_ref[...] + jnp.sum(onehot, axis=0, keepdims=True)
    carry_ref[...] = carry
    cnt_ref[...] = carry

    idx_out = jnp.zeros(l.shape, F32)
    gate_out = jnp.zeros(l.shape, F32)
    rank_out = jnp.zeros(l.shape, F32)
    for kk in range(TOP_K):
        rk = jnp.sum(jnp.where(sels[kk], cum, 0.0), axis=-1, keepdims=True)
        idx_out = jnp.where(lane == kk, idxs[kk], idx_out)
        gate_out = jnp.where(lane == kk, es[kk] / den, gate_out)
        rank_out = jnp.where(lane == kk, rk, rank_out)
    idx_ref[...] = idx_out.T[0:SUBLANES, :].astype(jnp.int32)
    gate_ref[...] = gate_out
    rank_ref[...] = rank_out.T[0:SUBLANES, :].astype(jnp.int32)


def router(logits, tm):
    t = logits.shape[0]
    spec = pl.BlockSpec((tm, LANES), lambda i: (i, 0))
    kspec = pl.BlockSpec((SUBLANES, tm), lambda i: (0, i))
    return pl.pallas_call(
        functools.partial(_router_kernel, tm=tm),
        grid=(t // tm,),
        in_specs=[spec],
        out_specs=[kspec, spec, kspec, pl.BlockSpec((1, LANES), lambda i: (0, 0))],
        out_shape=[jax.ShapeDtypeStruct((SUBLANES, t), jnp.int32), jax.ShapeDtypeStruct((t, LANES), F32),
                   jax.ShapeDtypeStruct((SUBLANES, t), jnp.int32), jax.ShapeDtypeStruct((1, LANES), F32)],
        scratch_shapes=[pltpu.VMEM((1, LANES), F32)],
        compiler_params=_cp(("arbitrary",)),
        name="router",
    )(logits)


def _dispatch_kernel(dest_ref, zrow_ref, meta_ref, x_ref, rows_ref, zero_ref, zsem, sem, *, tm, ne, n_sub_total,
                     n_tok):
    i = pl.program_id(0)

    @pl.when(i == 0)
    def _():
        zero_ref[...] = jnp.zeros_like(zero_ref)

        def zcopy(row0):
            return pltpu.make_async_copy(
                zero_ref, rows_ref.at[pl.ds(pl.multiple_of(row0, SUB_ROWS), SUB_ROWS), :], zsem)

        for e in range(ne):
            zcopy(zrow_ref[e]).start()

        def tail(sb, carry):
            zcopy(sb * SUB_ROWS).start()
            return carry

        lax.fori_loop(meta_ref[0], n_sub_total, tail, 0)
        for e in range(ne):
            zcopy(zrow_ref[e]).wait()

        def tail_wait(sb, carry):
            zcopy(sb * SUB_ROWS).wait()
            return carry

        lax.fori_loop(meta_ref[0], n_sub_total, tail_wait, 0)

    base = i * tm

    def row_copy(r, dst_row):
        return pltpu.make_async_copy(x_ref.at[pl.ds(r, 1), :], rows_ref.at[pl.ds(dst_row, 1), :], sem)

    for r in range(tm):
        for kk in range(TOP_K):
            row_copy(r, dest_ref[base + (kk * n_tok + r)]).start(priority=kk % 2)
    for kk in range(TOP_K):
        pltpu.make_async_copy(x_ref, rows_ref.at[pl.ds(0, tm), :], sem).wait()


def dispatch(xn, dest_flat, zrow, used_sub, n_rows, tm):
    t, d = xn.shape
    ne = zrow.shape[0]
    grid_spec = pltpu.PrefetchScalarGridSpec(
        num_scalar_prefetch=3,
        grid=(t // tm,),
        in_specs=[pl.BlockSpec((tm, d), lambda i, *_: (i, 0))],
        out_specs=pl.BlockSpec(memory_space=pl.ANY),
        scratch_shapes=[pltpu.VMEM((SUB_ROWS, d), xn.dtype), pltpu.SemaphoreType.DMA, pltpu.SemaphoreType.DMA],
    )
    return pl.pallas_call(
        functools.partial(_dispatch_kernel, tm=tm, ne=ne, n_sub_total=n_rows // SUB_ROWS, n_tok=t),
        grid_spec=grid_spec,
        out_shape=jax.ShapeDtypeStruct((n_rows, d), xn.dtype),
        compiler_params=_cp(("arbitrary",)),
        name="dispatch",
    )(dest_flat, zrow, used_sub, xn)


def _expert_kernel(ge_ref, r0_ref, ns_ref, meta_ref, rows_ref, wg_ref, bg_ref, wu_ref, bu_ref, wd_ref, bd_ref,
                   out_ref, stage_ref, xb_ref, acc_ref, ost_ref, xsem, osem, *, n_sub_total):
    g = pl.program_id(0)
    j = pl.program_id(1)
    nj = pl.num_programs(1)
    ng = meta_ref[1]
    used = g < ng
    nsub = ns_ref[g]
    d = stage_ref.shape[1]

    def x_copy(grp, sidx):
        row0 = pl.multiple_of(r0_ref[grp] + sidx * SUB_ROWS, SUB_ROWS)
        return pltpu.make_async_copy(rows_ref.at[pl.ds(row0, SUB_ROWS), :],
                                     stage_ref.at[pl.ds(sidx * SUB_ROWS, SUB_ROWS), :], xsem)

    def o_copy(row0, sidx):
        return pltpu.make_async_copy(ost_ref.at[pl.ds(sidx * SUB_ROWS, SUB_ROWS), :],
                                     out_ref.at[pl.ds(pl.multiple_of(row0, SUB_ROWS), SUB_ROWS), :], osem)

    def for_subs(count, fn):
        for sidx in range(GROUP_SUBS):
            @pl.when(sidx < count)
            def _():
                fn(sidx)

    @pl.when((g == 0) & (j == 0))
    def _():
        for_subs(ns_ref[0], lambda sidx: x_copy(0, sidx).start())
        ost_ref[0:SUB_ROWS, :] = jnp.zeros((SUB_ROWS, d), F32)

        def tail(sb, carry):
            cp = o_copy(sb * SUB_ROWS, 0)
            cp.start()
            cp.wait()
            return carry

        lax.fori_loop(meta_ref[0], n_sub_total, tail, 0)

    @pl.when(used & (j == 0))
    def _():
        for_subs(nsub, lambda sidx: x_copy(g, sidx).wait())

        def stage_in(sidx):
            sl = pl.ds(sidx * SUB_ROWS, SUB_ROWS)
            xb_ref[sl, :] = stage_ref[sl, :].astype(BF16)
            acc_ref[sl, :] = jnp.zeros((SUB_ROWS, d), F32)

        for_subs(nsub, stage_in)

        @pl.when(g + 1 < ng)
        def _():
            for_subs(ns_ref[g + 1], lambda sidx: x_copy(g + 1, sidx).start())

    @pl.when(used)
    def _():
        for n in range(1, GROUP_SUBS + 1):
            @pl.when(nsub == n)
            def _():
                rows = n * SUB_ROWS
                xb = xb_ref[0:rows, :]
                hg = jnp.minimum(_mm(xb, wg_ref[...].astype(BF16)) + bg_ref[...], SWIGLU_LIMIT)
                hu = jnp.clip(_mm(xb, wu_ref[...].astype(BF16)) + bu_ref[...], -SWIGLU_LIMIT, SWIGLU_LIMIT)
                hh = (hu + 1.0) * hg * _sigmoid(SWIGLU_ALPHA * hg)
                acc_ref[0:rows, :] += _mm(hh.astype(BF16), wd_ref[...].astype(BF16))

    @pl.when(used & (j == nj - 1))
    def _():
        @pl.when(g > 0)
        def _():
            for_subs(ns_ref[g - 1], lambda sidx: o_copy(r0_ref[g - 1] + sidx * SUB_ROWS, sidx).wait())

        def result_out(sidx):
            sl = pl.ds(sidx * SUB_ROWS, SUB_ROWS)
            ost_ref[sl, :] = acc_ref[sl, :] + bd_ref[...]
            o_copy(r0_ref[g] + sidx * SUB_ROWS, sidx).start()

        for_subs(nsub, result_out)

        @pl.when(g == ng - 1)
        def _():
            for_subs(nsub, lambda sidx: o_copy(r0_ref[g] + sidx * SUB_ROWS, sidx).wait())


def experts(rows, g_expert, g_row0, g_nsub, meta, w_gate, b_gate, w_up, b_up, w_down, b_down, tn):
    n_rows, d = rows.shape
    ne, _, de = w_gate.shape
    n_groups_max = g_expert.shape[0]
    nj = de // tn
    gr = GROUP_SUBS * SUB_ROWS

    def eb(g, ge, meta):
        return ge[jnp.minimum(g, meta[1] - 1)]

    def jb(g, j, meta):
        return jnp.where(g < meta[1], j, nj - 1)

    grid_spec = pltpu.PrefetchScalarGridSpec(
        num_scalar_prefetch=4,
        grid=(n_groups_max, nj),
        in_specs=[
            pl.BlockSpec(memory_space=pl.ANY),
            pl.BlockSpec((None, d, tn), lambda g, j, ge, r0, ns, mt: (eb(g, ge, mt), 0, jb(g, j, mt))),
            pl.BlockSpec((None, 1, tn), lambda g, j, ge, r0, ns, mt: (eb(g, ge, mt), 0, jb(g, j, mt))),
            pl.BlockSpec((None, d, tn), lambda g, j, ge, r0, ns, mt: (eb(g, ge, mt), 0, jb(g, j, mt))),
            pl.BlockSpec((None, 1, tn), lambda g, j, ge, r0, ns, mt: (eb(g, ge, mt), 0, jb(g, j, mt))),
            pl.BlockSpec((None, tn, d), lambda g, j, ge, r0, ns, mt: (eb(g, ge, mt), jb(g, j, mt), 0)),
            pl.BlockSpec((None, 1, d), lambda g, j, ge, r0, ns, mt: (eb(g, ge, mt), 0, 0)),
        ],
        out_specs=pl.BlockSpec(memory_space=pl.ANY),
        scratch_shapes=[
            pltpu.VMEM((gr, d), F32),
            pltpu.VMEM((gr, d), BF16),
            pltpu.VMEM((gr, d), F32),
            pltpu.VMEM((gr, d), F32),
            pltpu.SemaphoreType.DMA,
            pltpu.SemaphoreType.DMA,
        ],
    )
    return pl.pallas_call(
        functools.partial(_expert_kernel, n_sub_total=n_rows // SUB_ROWS),
        grid_spec=grid_spec,
        out_shape=jax.ShapeDtypeStruct((n_rows, d), F32),
        compiler_params=_cp(("arbitrary", "arbitrary")),
        name="experts",
    )(g_expert, g_row0, g_nsub, meta, rows, w_gate, b_gate.reshape(ne, 1, de), w_up, b_up.reshape(ne, 1, de),
      w_down, b_down.reshape(ne, 1, d))


def _combine_kernel(dest_ref, rows_ref, h_ref, gate_ref, g_ref, o_ref, buf_ref, sem, *, tm, n_tok):
    i = pl.program_id(0)
    nt = pl.num_programs(0)
    n = tm * TOP_K
    slot = i % 2

    def issue_tile(tile, slt):
        base = tile * tm
        for r in range(tm):
            for kk in range(TOP_K):
                pltpu.make_async_copy(rows_ref.at[pl.ds(dest_ref[base + (kk * n_tok + r)], 1), :],
                                      buf_ref.at[slt, pl.ds(kk * tm + r, 1), :], sem.at[slt]
                                      ).start(priority=kk % 2)

    @pl.when(i == 0)
    def _():
        issue_tile(0, 0)

    for par in range(2):
        @pl.when(slot == par)
        def _():
            pltpu.make_async_copy(rows_ref.at[pl.ds(0, n), :], buf_ref.at[par], sem.at[par]).wait()

            @pl.when(i + 1 < nt)
            def _():
                issue_tile(i + 1, 1 - par)

    gates = gate_ref[...]
    h = h_ref[...]
    for kk in range(TOP_K):
        h = h + gates[:, kk:kk + 1] * buf_ref[slot, kk * tm:(kk + 1) * tm, :]
    ms = jnp.mean(h * h, axis=-1, keepdims=True)
    o_ref[...] = (h * lax.rsqrt(ms + EPS) * g_ref[...]).astype(o_ref.dtype)


def combine(dest_flat, out_rows, h1, gates, g_final, tm, out_dtype):
    t, d = h1.shape
    grid_spec = pltpu.PrefetchScalarGridSpec(
        num_scalar_prefetch=1,
        grid=(t // tm,),
        in_specs=[
            pl.BlockSpec(memory_space=pl.ANY),
            pl.BlockSpec((tm, d), lambda i, *_: (i, 0)),
            pl.BlockSpec((tm, LANES), lambda i, *_: (i, 0)),
            pl.BlockSpec((1, d), lambda i, *_: (0, 0)),
        ],
        out_specs=pl.BlockSpec((tm, d), lambda i, *_: (i, 0)),
        scratch_shapes=[pltpu.VMEM((2, tm * TOP_K, d), out_rows.dtype), pltpu.SemaphoreType.DMA((2,))],
    )
    return pl.pallas_call(
        functools.partial(_combine_kernel, tm=tm, n_tok=t),
        grid_spec=grid_spec,
        out_shape=jax.ShapeDtypeStruct((t, d), out_dtype),
        compiler_params=_cp(("arbitrary",)),
        name="combine",
    )(dest_flat, out_rows, h1, gates, g_final.reshape(1, d).astype(F32))


def _tile(n, pref):
    while n % pref:
        pref //= 2
    return pref


def _group_tables(counts, n_groups_max):
    ne = counts.shape[0]
    nsub_e = (counts + SUB_ROWS - 1) // SUB_ROWS
    pend_sub = jnp.cumsum(nsub_e)
    pstart = (pend_sub - nsub_e) * SUB_ROWS
    ng_e = (nsub_e + GROUP_SUBS - 1) // GROUP_SUBS
    base_e = nsub_e // jnp.maximum(ng_e, 1)
    rem_e = nsub_e - base_e * ng_e
    cum_ng = jnp.cumsum(ng_e)
    n_groups = cum_ng[-1]
    g = jnp.arange(n_groups_max, dtype=jnp.int32)
    e_g = jnp.minimum(jnp.sum((g[:, None] >= cum_ng[None, :]).astype(jnp.int32), axis=1), ne - 1)
    lg = g - (cum_ng[e_g] - ng_e[e_g])
    valid = g < n_groups
    g_nsub = jnp.where(valid, base_e[e_g] + (lg < rem_e[e_g]).astype(jnp.int32), 0)
    g_row0 = jnp.where(valid, pstart[e_g] + SUB_ROWS * (lg * base_e[e_g] + jnp.minimum(lg, rem_e[e_g])), 0)
    zrow = pstart + jnp.maximum(nsub_e - 1, 0) * SUB_ROWS
    meta = jnp.stack([pend_sub[-1], n_groups]).astype(jnp.int32)
    i32 = lambda z: z.astype(jnp.int32)
    return i32(pstart), i32(zrow), i32(e_g), i32(g_row0), i32(g_nsub), meta


def kernel(x, mem, g_mix, w_in, mu_rwkv, rwkv_w0, rwkv_w2, rwkv_a0, rwkv_a2, rwkv_g2, rwkv_k_k, rwkv_k_a,
           rwkv_r_k, rwkv_ln_w, rwkv_ln_b, w_rwkv_out, lru_conv_w, lru_conv_b, lru_w_r, lru_b_r, lru_w_i,
           lru_b_i, lru_lambda, w_lru_out, g_mem, w_mem_k, w_mem_v, w_xa_out, w_o, g_ffn, w_router,
           b_router, w_gate, b_gate, w_up, b_up, w_down, b_down, g_final):
    bsz, seq, d = x.shape
    assert w_in.shape[0] == 1, "single-layer block"
    l = 0
    t = bsz * seq
    c = rwkv_w0.shape[1]
    n_lora = LORA_W + LORA_A + LORA_G
    ne = w_router.shape[2]
    assert lru_conv_b.shape[1] == c and w_mem_k.shape[2] == c and c % LORA_PAD == 0

    h = x.reshape(t, d).astype(F32)
    memf = mem.reshape(bsz * mem.shape[1], d)
    wi = w_in[l].astype(BF16)
    c0 = 3 * c + n_lora
    w_a = jnp.concatenate(
        [wi[:, 0:3 * c], wi[:, c0:c0 + 2 * c], wi[:, 3 * c:c0], jnp.zeros((d, LORA_PAD - n_lora), BF16)], axis=1)
    w_b = wi[:, c0 + 2 * c:]

    xn = norm_cast(h, g_mix[l], _tile(t, 512))
    p_a = matmul(xn, w_a, _tile(t, 1024), _tile(w_a.shape[1], 512), F32)
    p_b = matmul(xn, w_b, _tile(t, 1024), _tile(w_b.shape[1], 1024), BF16)
    mn = norm_cast(memf, g_mem[l], _tile(memf.shape[0], 256))
    kv = matmul(mn, jnp.concatenate([w_mem_k[l], w_mem_v[l]], axis=1).astype(BF16),
                _tile(memf.shape[0], 256), _tile(2 * c, 512), BF16)

    y_a = rwkv_branch(p_a, bsz, seq, mu_rwkv[l], rwkv_w0[l], rwkv_w2[l], rwkv_a0[l], rwkv_a2[l],
                      rwkv_g2[l], rwkv_k_k[l], rwkv_k_a[l], rwkv_r_k[l].reshape(-1), rwkv_ln_w[l],
                      rwkv_ln_b[l], _tile(seq, 256), 5 * c)
    y_b = lru_branch(p_a, bsz, seq, 3 * c, 4 * c, lru_conv_w[l], lru_conv_b[l], lru_w_r[l], lru_b_r[l],
                     lru_w_i[l], lru_b_i[l], lru_lambda[l], _tile(seq, 256))
    y_c = xattn_branch(p_b, kv, bsz, seq, _tile(seq, 512))
    merged = merge(y_a, y_b, y_c, w_rwkv_out[l].astype(BF16), w_lru_out[l].astype(BF16),
                   w_xa_out[l].astype(BF16), p_b, c, _tile(t, 512), _tile(d, 1024))
    h1, xn2, logits = wo_block(h, merged, w_o[l], g_ffn[l], w_router[l], b_router[l], _tile(t, 256))

    idx, gates, rank, counts = router(logits, _tile(t, 512))
    counts = counts[0, :ne].astype(jnp.int32)
    n_rows = t * TOP_K + ne * SUB_ROWS
    n_groups_max = (t * TOP_K) // (GROUP_SUBS * SUB_ROWS) + ne
    pstart, zrow, g_expert, g_row0, g_nsub, meta = _group_tables(counts, n_groups_max)
    e_kt = idx[:TOP_K]
    first_row = jnp.zeros_like(e_kt)
    for e in range(ne):
        first_row = jnp.where(e_kt == e, pstart[e], first_row)
    dest = (first_row + rank[:TOP_K]).reshape(-1)

    tmd = _tile(t, 256)
    rows = dispatch(xn2, dest, zrow, meta, n_rows, tmd)
    out_rows = experts(rows, g_expert, g_row0, g_nsub, meta, w_gate[l], b_gate[l], w_up[l], b_up[l],
                       w_down[l], b_down[l], _tile(w_gate.shape[3], 256))
    out = combine(dest, out_rows, h1, gates, g_final, tmd, x.dtype)
    return out.reshape(bsz, seq, d)
```

```python
import functools

import jax
import jax.numpy as jnp
from jax import lax
from jax.experimental import pallas as pl
from jax.experimental.pallas import tpu as pltpu

F32 = jnp.float32
BF16 = jnp.bfloat16

EPS = 1e-6
GN_EPS = 64e-5
RG_C = 8.0
SWIGLU_LIMIT = 7.0
SWIGLU_ALPHA = 1.702
RWKV_HEAD = 64
LORA_W = 64
LORA_A = 64
LORA_G = 160
CONV_W = 4
LRU_BLOCKS = 16
XA_HEADS = 4
TOP_K = 4
N_BRANCH = 3

LANES = 128
SUBLANES = 8
CHUNK = 64
SUB = 16
SUB_ROWS = 256
GROUP_SUBS = 4
LORA_PAD = 512
NEG_BIG = -1e30
VMEM_LIMIT = 56 * 1024 * 1024


def _cp(sem, vmem=None):
    return pltpu.CompilerParams(dimension_semantics=sem, vmem_limit_bytes=vmem or VMEM_LIMIT)


def _mm(a, b):
    return jnp.dot(a, b, preferred_element_type=F32)


def _mm_nt(a, b):
    return lax.dot_general(a, b, (((1,), (1,)), ((), ())), preferred_element_type=F32)


def _mm_tn(a, b):
    return lax.dot_general(a, b, (((0,), (0,)), ((), ())), preferred_element_type=F32)


def _split(x):
    hi = x.astype(BF16)
    lo = (x - hi.astype(F32)).astype(BF16)
    return hi, lo


def _mm_exact_rhs(x, m):
    hi, lo = _split(x)
    return _mm(hi, m) + _mm(lo, m)


def _mm_exact_lhs(m, x):
    hi, lo = _split(x)
    return _mm(m, hi) + _mm(m, lo)


def _sigmoid(x):
    return 1.0 / (1.0 + jnp.exp(-x))


def _softplus(x):
    return jnp.maximum(x, 0.0) + jnp.log(1.0 + jnp.exp(-jnp.abs(x)))


def _norm_kernel(x_ref, g_ref, o_ref):
    x = x_ref[...].astype(F32)
    ms = jnp.mean(x * x, axis=-1, keepdims=True)
    o_ref[...] = (x * lax.rsqrt(ms + EPS) * g_ref[...]).astype(o_ref.dtype)


def norm_cast(x, g, tm, out_dtype=BF16):
    t, d = x.shape
    return pl.pallas_call(
        _norm_kernel,
        grid=(t // tm,),
        in_specs=[pl.BlockSpec((tm, d), lambda i: (i, 0)), pl.BlockSpec((1, d), lambda i: (0, 0))],
        out_specs=pl.BlockSpec((tm, d), lambda i: (i, 0)),
        out_shape=jax.ShapeDtypeStruct((t, d), out_dtype),
        compiler_params=_cp(("parallel",)),
        name="norm_cast",
    )(x, g.reshape(1, d).astype(F32))


def _matmul_kernel(a_ref, w_ref, o_ref):
    o_ref[...] = _mm(a_ref[...], w_ref[...]).astype(o_ref.dtype)


def matmul(a, w, tm, tn, out_dtype):
    m, k = a.shape
    n = w.shape[1]
    return pl.pallas_call(
        _matmul_kernel,
        grid=(m // tm, n // tn),
        in_specs=[pl.BlockSpec((tm, k), lambda i, j: (i, 0)), pl.BlockSpec((k, tn), lambda i, j: (0, j))],
        out_specs=pl.BlockSpec((tm, tn), lambda i, j: (i, j)),
        out_shape=jax.ShapeDtypeStruct((m, n), out_dtype),
        compiler_params=_cp(("parallel", "parallel")),
        name="matmul",
    )(a, w)


def _norm_matmul_kernel(x_ref, g_ref, w_ref, o_ref, xn_ref):
    @pl.when(pl.program_id(1) == 0)
    def _():
        x = x_ref[...].astype(F32)
        ms = jnp.mean(x * x, axis=-1, keepdims=True)
        xn_ref[...] = (x * lax.rsqrt(ms + EPS) * g_ref[...]).astype(BF16)

    o_ref[...] = _mm(xn_ref[...], w_ref[...]).astype(o_ref.dtype)


def norm_matmul(x, g, w, tm, tn, out_dtype):
    m, k = x.shape
    n = w.shape[1]
    return pl.pallas_call(
        _norm_matmul_kernel,
        grid=(m // tm, n // tn),
        in_specs=[pl.BlockSpec((tm, k), lambda i, j: (i, 0)), pl.BlockSpec((1, k), lambda i, j: (0, 0)),
                  pl.BlockSpec((k, tn), lambda i, j: (0, j))],
        out_specs=pl.BlockSpec((tm, tn), lambda i, j: (i, j)),
        out_shape=jax.ShapeDtypeStruct((m, n), out_dtype),
        scratch_shapes=[pltpu.VMEM((tm, k), BF16)],
        compiler_params=_cp(("parallel", "arbitrary")),
        name="norm_matmul",
    )(x, g.reshape(1, k).astype(F32), w)


def _rwkv_kernel(r_ref, k_ref, v_ref, lo_ref, rp_ref, kp_ref, vp_ref, lop_ref,
                 mur_ref, muk_ref, muv_ref, mulo_ref, w0_ref, a0_ref, kk_ref, ka_ref, rk_ref,
                 lnw_ref, lnb_ref, w2_ref, a2_ref, g2_ref, o_ref,
                 s_ref, rpq_ref, y0q_ref, abq_ref, gtq_ref, peq_ref, gq_ref, bnq_ref,
                 ra_ref, va_ref, k2a_ref, ava_ref, bva_ref, lda_ref, ca_ref, ga_ref, bna_ref, *, tm, nb):
    i = pl.program_id(1)
    first = i == 0
    a_refs = (ra_ref, va_ref, k2a_ref, ava_ref, bva_ref, lda_ref, ca_ref, ga_ref, bna_ref)

    @pl.when(first)
    def _():
        for ref in (s_ref, rpq_ref, y0q_ref, abq_ref, gtq_ref, peq_ref, gq_ref, bnq_ref) + a_refs:
            ref[...] = jnp.zeros_like(ref)

    bf = lambda z: z.astype(BF16)
    bs = range(nb)
    n2 = 2 * CHUNK
    nch = tm // CHUNK
    items = [(b, q) for q in range(nch) for b in bs]
    rng = range(len(items))
    inv_n = 1.0 / RWKV_HEAD

    lane = lax.broadcasted_iota(jnp.int32, (LANES, LANES), 1)
    sub = lax.broadcasted_iota(jnp.int32, (LANES, LANES), 0)
    e2 = jnp.where((lane // RWKV_HEAD) == (sub // RWKV_HEAD), 1.0, 0.0).astype(BF16)

    def mix(x_ref, p_ref, mu_ref, b):
        x = x_ref[b]
        prev = jnp.where(first, 0.0, p_ref[b, SUBLANES - 1:SUBLANES, :])
        row = lax.broadcasted_iota(jnp.int32, x.shape, 0)
        sh = jnp.where(row == 0, prev, pltpu.roll(x, 1, axis=0))
        return x + (sh - x) * mu_ref[...]

    def recurrence():
        s = [s_ref[b] for b in bs]
        ys = [[] for _ in bs]
        for n, (b, q) in enumerate(items):
            sb = bf(s[b])
            yst = _mm_nt(rpq_ref[n], sb) + y0q_ref[n]
            s[b] = s[b] * peq_ref[n, 0:1, :] + _mm_nt(sb, abq_ref[n]) + gtq_ref[n]
            ys[b].append(yst[0:CHUNK] + yst[CHUNK:n2])
            yield
        for b in bs:
            s_ref[b] = s[b]
        y = [jnp.concatenate(z, axis=0) if len(z) > 1 else z[0] for z in ys]
        mean = [_mm_exact_rhs(z, e2) * inv_n for z in y]
        yield
        yc = [y[b] - mean[b] for b in bs]
        var = [_mm_exact_rhs(z * z, e2) * inv_n for z in yc]
        yield
        for b in bs:
            yn = yc[b] * lax.rsqrt(var[b] + GN_EPS) * lnw_ref[...] + lnb_ref[...]
            o_ref[b] = ((yn + bnq_ref[b]) * gq_ref[b]).astype(o_ref.dtype)

    def prologue():
        r = [mix(r_ref, rp_ref, mur_ref, b) for b in bs]
        yield
        k = [mix(k_ref, kp_ref, muk_ref, b) for b in bs]
        yield
        v = [mix(v_ref, vp_ref, muv_ref, b) for b in bs]
        yield
        lo = [mix(lo_ref, lop_ref, mulo_ref, b) for b in bs]
        yield
        wl = [_mm(bf(jnp.tanh(z[:, 0:LANES])), w2_ref[...]) for z in lo]
        al = [_mm(bf(z[:, 0:LANES]), a2_ref[...]) for z in lo]
        yield
        g = [_mm(bf(_sigmoid(z[:, LANES:3 * LANES])), g2_ref[...]) for z in lo]
        yield
        ld = [-jnp.exp(-_softplus(-(w0_ref[...] + z)) - 0.5) for z in wl]
        a_sig = [_sigmoid(a0_ref[...] + z) for z in al]
        yield
        kk = [z * kk_ref[...] for z in k]
        kk = [z * lax.rsqrt(jnp.maximum(_mm_exact_rhs(z * z, e2), 1e-24)) for z in kk]
        yield
        k2 = [k[b] * (1.0 + (a_sig[b] - 1.0) * ka_ref[...]) for b in bs]
        bonus = [_mm_exact_rhs(r[b] * k2[b] * rk_ref[...], e2) * v[b] for b in bs]
        yield
        trow = lax.broadcasted_iota(jnp.int32, (tm, tm), 0)
        tcol = lax.broadcasted_iota(jnp.int32, (tm, tm), 1)
        tri = jnp.where(((trow // CHUNK) == (tcol // CHUNK)) & (tcol <= trow), 1.0, 0.0).astype(BF16)
        c = [_mm_exact_lhs(tri, z) for z in ld]
        yield
        for b in bs:
            vals = (r[b], v[b], k2[b], -kk[b], kk[b] * a_sig[b], ld[b], c[b], g[b], bonus[b])
            for ref, val in zip(a_refs, vals):
                ref[b] = val

    r, v, k2, a_vec, b_vec, ld, c, g, bonus = ([ref[b] for b in bs] for ref in a_refs)

    lane_row = lax.broadcasted_iota(jnp.int32, (CHUNK, LANES), 1)
    head_a = lane_row < RWKV_HEAD

    def stack(x):
        return jnp.concatenate([jnp.where(head_a, x, 0.0), jnp.where(head_a, 0.0, x)], axis=0)

    ri = lax.broadcasted_iota(jnp.int32, (n2, n2), 0)
    ci = lax.broadcasted_iota(jnp.int32, (n2, n2), 1)
    strict = ri > ci
    incl = ri >= ci
    blk = (ri // SUB) == (ci // SUB)
    eye = jnp.where(ri == ci, 1.0, 0.0)

    sl = lambda q: slice(q * CHUNK, (q + 1) * CHUNK)
    cs = [c[b][sl(q)] for b, q in items]
    lds = [ld[b][sl(q)] for b, q in items]
    c_ends = [z[CHUNK - 1:CHUNK, :] for z in cs]
    e_pos = [jnp.exp(z) for z in cs]
    e_neg = [jnp.exp(-z) for z in cs]
    e_end = [jnp.exp(ce - z) for ce, z in zip(c_ends, cs)]
    p_end = [jnp.exp(ce) for ce in c_ends]
    rt32 = [stack(r[b][sl(q)] * e_pos[n]) for n, (b, q) in enumerate(items)]
    at = [bf(stack(a_vec[b][sl(q)] * jnp.exp(cs[n] - lds[n]))) for n, (b, q) in enumerate(items)]
    rt = [bf(z) for z in rt32]
    bt = [bf(stack(b_vec[b][sl(q)] * e_neg[n])) for n, (b, q) in enumerate(items)]
    kt = [bf(stack(k2[b][sl(q)] * e_neg[n])) for n, (b, q) in enumerate(items)]
    bh = [bf(stack(b_vec[b][sl(q)] * e_end[n])) for n, (b, q) in enumerate(items)]
    kh = [bf(stack(k2[b][sl(q)] * e_end[n])) for n, (b, q) in enumerate(items)]
    vs = [bf(stack(v[b][sl(q)])) for b, q in items]

    ops = {}

    def chunk_operators():
        gram = [_mm_nt(jnp.concatenate([at[n], rt[n]], axis=0), jnp.concatenate([bt[n], kt[n]], axis=0))
                for n in rng]
        yield
        l_ab = [jnp.where(strict, z[0:n2, 0:n2], 0.0) for z in gram]
        l_ak = [bf(jnp.where(strict, z[0:n2, n2:], 0.0)) for z in gram]
        m_rb = [bf(jnp.where(incl, z[n2:, 0:n2], 0.0)) for z in gram]
        m_rk = [bf(jnp.where(incl, z[n2:, n2:], 0.0)) for z in gram]
        yield
        d = [jnp.where(blk, z, 0.0) for z in l_ab]
        l_off = [bf(l_ab[n] - d[n]) for n in rng]
        db = [bf(z) for z in d]
        p = [eye + z for z in d]
        yield
        d2 = [bf(_mm(z, z)) for z in db]
        yield
        lv = [_mm(l_ak[n], vs[n]) for n in rng]
        yield
        p = [p[n] + _mm(d2[n], bf(p[n])) for n in rng]
        yield
        d4 = [bf(_mm(z, z)) for z in d2]
        yield
        p = [p[n] + _mm(d4[n], bf(p[n])) for n in rng]
        yield
        d8 = [bf(_mm(z, z)) for z in d4]
        yield
        td = [bf(p[n] + _mm(d8[n], bf(p[n]))) for n in rng]
        yield
        mm = [bf(_mm(td[n], l_off[n])) for n in rng]
        yield
        x = [_mm(td[n], jnp.concatenate([at[n], bf(lv[n])], axis=1)) for n in rng]
        yield
        m2 = [bf(_mm(z, z)) for z in mm]
        yield
        x = [x[n] + _mm(mm[n], bf(x[n])) for n in rng]
        yield
        xb = [bf(x[n] + _mm(m2[n], bf(x[n]))) for n in rng]
        wb = [z[:, 0:n2] for z in xb]
        u0b = [z[:, n2:] for z in xb]
        yield
        t1 = [_mm(m_rb[n], xb[n]) for n in rng]
        yield
        rp = [bf(rt32[n] + t1[n][:, 0:n2]) for n in rng]
        y0 = [t1[n][:, n2:] + _mm(m_rk[n], vs[n]) for n in rng]
        yield
        ab = [bf(_mm_tn(bh[n], wb[n])) for n in rng]
        yield
        gt = [_mm_tn(u0b[n], bh[n]) + _mm_tn(vs[n], kh[n]) for n in rng]
        ops.update(rp=rp, y0=y0, ab=ab, gt=gt)

    gens = [[chunk_operators(), 19, 0], [prologue(), 10, 0], [recurrence(), len(items) + 2, 0]]
    while gens:
        entry = min(gens, key=lambda e: e[2] / e[1])
        entry[2] += 1
        if next(entry[0], "done") == "done":
            gens.remove(entry)

    for n in rng:
        rpq_ref[n] = ops["rp"][n]
        y0q_ref[n] = ops["y0"][n]
        abq_ref[n] = ops["ab"][n]
        gtq_ref[n] = ops["gt"][n]
        peq_ref[n] = jnp.broadcast_to(p_end[n], (SUBLANES, LANES))
    for b in bs:
        gq_ref[b] = g[b]
        bnq_ref[b] = bonus[b]


def rwkv_branch(p_a, bsz, seq, mu, w0, w2, a0, a2, g2, k_k, k_a, r_k, ln_w, ln_b, tm, col0_lora):
    t, na = p_a.shape
    c = w0.shape[0]
    nj = c // LANES
    nt = seq // tm
    lw = LORA_PAD
    lora_blk = col0_lora // lw
    assert col0_lora % lw == 0 and seq % tm == 0 and tm % CHUNK == 0
    p3 = p_a.reshape(bsz, seq, na)

    mu_rkv = mu[:3 * c].reshape(1, 3 * c)
    mu_lo = jnp.pad(mu[3 * c:], (0, lw - (LORA_W + LORA_A + LORA_G))).reshape(1, lw)
    w2p = jnp.pad(w2, ((0, LANES - LORA_W), (0, 0))).astype(BF16)
    a2p = jnp.pad(a2, ((LORA_W, LANES - LORA_W - LORA_A), (0, 0))).astype(BF16)
    g2p = jnp.pad(g2, ((0, 2 * LANES - LORA_G), (0, 0))).astype(BF16)
    vec = lambda z: z.reshape(1, c).astype(F32)

    rpb = tm // SUBLANES

    tile = lambda i: jnp.minimum(i, nt - 1)
    prow = lambda i: jnp.maximum(tile(i) * rpb - 1, 0)

    def cur(off):
        return pl.BlockSpec((bsz, tm, LANES), lambda j, i: (0, tile(i), off * nj + j))

    def prev(off):
        return pl.BlockSpec((bsz, SUBLANES, LANES), lambda j, i: (0, prow(i), off * nj + j))

    def slab(rows):
        return pl.BlockSpec((rows, LANES), lambda j, i: (0, j))

    n_items = bsz * (tm // CHUNK)
    n2 = 2 * CHUNK
    in_specs = [
        cur(0), cur(1), cur(2),
        pl.BlockSpec((bsz, tm, lw), lambda j, i: (0, tile(i), lora_blk)),
        prev(0), prev(1), prev(2),
        pl.BlockSpec((bsz, SUBLANES, lw), lambda j, i: (0, prow(i), lora_blk)),
        pl.BlockSpec((1, LANES), lambda j, i: (0, j)),
        pl.BlockSpec((1, LANES), lambda j, i: (0, nj + j)),
        pl.BlockSpec((1, LANES), lambda j, i: (0, 2 * nj + j)),
        pl.BlockSpec((1, lw), lambda j, i: (0, 0)),
        slab(1), slab(1), slab(1), slab(1), slab(1), slab(1), slab(1),
        slab(LANES), slab(LANES), slab(2 * LANES),
    ]
    out = pl.pallas_call(
        functools.partial(_rwkv_kernel, tm=tm, nb=bsz),
        grid=(nj, nt + 2),
        in_specs=in_specs,
        out_specs=pl.BlockSpec((bsz, tm, LANES), lambda j, i: (0, jnp.maximum(i - 2, 0), j)),
        out_shape=jax.ShapeDtypeStruct((bsz, seq, c), BF16),
        scratch_shapes=[
            pltpu.VMEM((bsz, n2, n2), F32),
            pltpu.VMEM((n_items, n2, n2), BF16),
            pltpu.VMEM((n_items, n2, n2), F32),
            pltpu.VMEM((n_items, n2, n2), BF16),
            pltpu.VMEM((n_items, n2, n2), F32),
            pltpu.VMEM((n_items, SUBLANES, LANES), F32),
            pltpu.VMEM((bsz, tm, LANES), F32),
            pltpu.VMEM((bsz, tm, LANES), F32),
        ] + [pltpu.VMEM((bsz, tm, LANES), F32)] * 9,
        compiler_params=_cp(("parallel", "arbitrary")),
        name="rwkv",
    )(p3, p3, p3, p3, p3, p3, p3, p3,
      mu_rkv, mu_rkv, mu_rkv, mu_lo, vec(w0), vec(a0), vec(k_k), vec(k_a), vec(r_k), vec(ln_w), vec(ln_b),
      w2p, a2p, g2p)
    return out.reshape(t, c)


def _lru_kernel(x_ref, xp_ref, y_ref, cw_ref, cb_ref, wr_ref, br_ref, wi_ref, bi_ref, sp_ref,
                o_ref, h_ref, *, tm, nq):
    i = pl.program_id(1)
    first = i == 0

    @pl.when(first)
    def _():
        h_ref[...] = jnp.zeros_like(h_ref)

    x = x_ref[...]
    prev = jnp.where(first, 0.0, xp_ref[...])
    xx = jnp.concatenate([prev, x], axis=0)
    xc = cb_ref[...] + x * cw_ref[0:1, :]
    for j in range(1, CONV_W):
        xc = xc + xx[SUBLANES - j:SUBLANES - j + tm, :] * cw_ref[j:j + 1, :]

    xcb = xc.astype(BF16)
    qw = x.shape[1] // nq
    gr = jnp.concatenate([_mm(xcb[:, q * qw:(q + 1) * qw], wr_ref[q]) for q in range(nq)], axis=1)
    gi = jnp.concatenate([_mm(xcb[:, q * qw:(q + 1) * qw], wi_ref[q]) for q in range(nq)], axis=1)
    gate_r = _sigmoid(gr + br_ref[...])
    gate_i = _sigmoid(gi + bi_ref[...])
    log_a = -RG_C * gate_r * sp_ref[...]
    a = jnp.exp(log_a)
    mult = jnp.sqrt(jnp.maximum(1.0 - jnp.exp(2.0 * log_a), 0.0))
    row = lax.broadcasted_iota(jnp.int32, x.shape, 0)
    mult = jnp.where(first & (row == 0), 1.0, mult)
    b = mult * gate_i * xc

    rin = row % SUBLANES
    sft = 1
    while sft < SUBLANES:
        keep = rin >= sft
        a_sh = jnp.where(keep, pltpu.roll(a, sft, axis=0), 1.0)
        b_sh = jnp.where(keep, pltpu.roll(b, sft, axis=0), 0.0)
        b = b + a * b_sh
        a = a * a_sh
        sft *= 2
    carry = h_ref[...]
    hs = []
    for gi in range(tm // SUBLANES):
        rows = slice(gi * SUBLANES, (gi + 1) * SUBLANES)
        hg = a[rows] * carry + b[rows]
        carry = hg[SUBLANES - 1:SUBLANES, :]
        hs.append(hg)
    h = jnp.concatenate(hs, axis=0)
    h_ref[...] = carry

    py = y_ref[...]
    gelu = 0.5 * py * (1.0 + jnp.tanh(0.7978845608028654 * (py + 0.044715 * py * py * py)))
    o_ref[...] = (h * gelu).astype(o_ref.dtype)


def lru_branch(p_a, bsz, seq, col_x, col_y, conv_w, conv_b, w_r, b_r, w_i, b_i, lam, tm):
    t = p_a.shape[0]
    c = conv_b.shape[0]
    nt = seq // tm
    rpb = tm // SUBLANES
    bx, by = col_x // c, col_y // c
    assert col_x % c == 0 and col_y % c == 0
    nblk, bw = w_r.shape[0], w_r.shape[1]
    per = max(1, (2 * LANES) // bw)
    nq = nblk // per

    def blockdiag(w):
        w = w.reshape(nq, per, bw, bw)
        eye = jnp.eye(per, dtype=w.dtype)
        return jnp.einsum('qpab,pr->qparb', w, eye).reshape(nq, per * bw, per * bw).astype(BF16)

    vec = lambda z: z.reshape(1, c).astype(F32)
    sp = jax.nn.softplus(-lam)
    return pl.pallas_call(
        functools.partial(_lru_kernel, tm=tm, nq=nq),
        grid=(bsz, nt),
        in_specs=[
            pl.BlockSpec((tm, c), lambda b, i: (b * nt + i, bx)),
            pl.BlockSpec((SUBLANES, c), lambda b, i: (jnp.maximum((b * nt + i) * rpb - 1, 0), bx)),
            pl.BlockSpec((tm, c), lambda b, i: (b * nt + i, by)),
            pl.BlockSpec((CONV_W, c), lambda b, i: (0, 0)),
            pl.BlockSpec((1, c), lambda b, i: (0, 0)),
            pl.BlockSpec((nq, per * bw, per * bw), lambda b, i: (0, 0, 0)),
            pl.BlockSpec((1, c), lambda b, i: (0, 0)),
            pl.BlockSpec((nq, per * bw, per * bw), lambda b, i: (0, 0, 0)),
            pl.BlockSpec((1, c), lambda b, i: (0, 0)),
            pl.BlockSpec((1, c), lambda b, i: (0, 0)),
        ],
        out_specs=pl.BlockSpec((tm, c), lambda b, i: (b * nt + i, 0)),
        out_shape=jax.ShapeDtypeStruct((t, c), BF16),
        scratch_shapes=[pltpu.VMEM((1, c), F32)],
        compiler_params=_cp(("parallel", "arbitrary")),
        name="lru",
    )(p_a, p_a, p_a, conv_w.astype(F32), vec(conv_b), blockdiag(w_r), vec(b_r), blockdiag(w_i), vec(b_i),
      vec(sp))


def _xattn_kernel(q_ref, k_ref, v_ref, o_ref, *, hd):
    scale = hd ** -0.5
    outs = []
    for h in range(XA_HEADS):
        sl = slice(h * hd, (h + 1) * hd)
        s = _mm_nt(q_ref[:, sl], k_ref[:, sl]) * scale
        m = jnp.max(s, axis=-1, keepdims=True)
        e = jnp.exp(s - m)
        pr = e / jnp.sum(e, axis=-1, keepdims=True)
        outs.append(_mm(pr.astype(BF16), v_ref[:, sl]))
    o_ref[...] = jnp.concatenate(outs, axis=1).astype(o_ref.dtype)


def xattn_branch(p_b, kv, bsz, seq, tm):
    t = p_b.shape[0]
    c = kv.shape[1] // 2
    mlen = kv.shape[0] // bsz
    nt = seq // tm
    return pl.pallas_call(
        functools.partial(_xattn_kernel, hd=c // XA_HEADS),
        grid=(bsz, nt),
        in_specs=[
            pl.BlockSpec((tm, c), lambda b, i: (b * nt + i, 0)),
            pl.BlockSpec((mlen, c), lambda b, i: (b, 0)),
            pl.BlockSpec((mlen, c), lambda b, i: (b, 1)),
        ],
        out_specs=pl.BlockSpec((tm, c), lambda b, i: (b * nt + i, 0)),
        out_shape=jax.ShapeDtypeStruct((t, c), BF16),
        compiler_params=_cp(("parallel", "parallel")),
        name="xattn",
    )(p_b, kv, kv)


def _merge_kernel(ya_ref, yb_ref, yc_ref, wa_ref, wb_ref, wc_ref, ga_ref, gb_ref, gc_ref, o_ref):
    acc = _sigmoid(ga_ref[...].astype(F32)) * _mm(ya_ref[...], wa_ref[...])
    acc = acc + _sigmoid(gb_ref[...].astype(F32)) * _mm(yb_ref[...], wb_ref[...])
    acc = acc + _sigmoid(gc_ref[...].astype(F32)) * _mm(yc_ref[...], wc_ref[...])
    o_ref[...] = acc.astype(o_ref.dtype)


def merge(y_a, y_b, y_c, w_a, w_b, w_c, p_b, gate_col0, tm, tn):
    t, c = y_a.shape
    d = w_a.shape[1]
    g0 = gate_col0 // tn
    gstep = d // tn
    assert gate_col0 % tn == 0
    yspec = pl.BlockSpec((tm, c), lambda i, j: (i, 0))
    wspec = pl.BlockSpec((c, tn), lambda i, j: (0, j))
    gspec = lambda n: pl.BlockSpec((tm, tn), lambda i, j: (i, g0 + n * gstep + j))
    return pl.pallas_call(
        _merge_kernel,
        grid=(t // tm, d // tn),
        in_specs=[yspec, yspec, yspec, wspec, wspec, wspec, gspec(0), gspec(1), gspec(2)],
        out_specs=pl.BlockSpec((tm, tn), lambda i, j: (i, j)),
        out_shape=jax.ShapeDtypeStruct((t, d), BF16),
        compiler_params=_cp(("parallel", "parallel")),
        name="merge",
    )(y_a, y_b, y_c, w_a, w_b, w_c, p_b, p_b, p_b)


def _wo_kernel(x_ref, m_ref, wo_ref, g_ref, wrh_ref, wrl_ref, br_ref, h_ref, xn_ref, lg_ref):
    h = x_ref[...].astype(F32) + _mm(m_ref[...], wo_ref[...])
    h_ref[...] = h
    ms = jnp.mean(h * h, axis=-1, keepdims=True)
    xn = h * lax.rsqrt(ms + EPS) * g_ref[...]
    xn_ref[...] = xn
    hi, lo = _split(xn)
    lg_ref[...] = _mm(hi, wrh_ref[...]) + _mm(lo, wrh_ref[...]) + _mm(hi, wrl_ref[...]) + br_ref[...]


def wo_block(x, merged, w_o, g_ffn, w_router, b_router, tm):
    t, d = x.shape
    ne = w_router.shape[1]
    wr = jnp.pad(w_router.astype(F32), ((0, 0), (0, LANES - ne)))
    wr_hi = wr.astype(BF16)
    wr_lo = (wr - wr_hi.astype(F32)).astype(BF16)
    br = jnp.pad(b_router.astype(F32), (0, LANES - ne), constant_values=NEG_BIG).reshape(1, LANES)
    row = pl.BlockSpec((tm, d), lambda i: (i, 0))
    full = lambda r, c: pl.BlockSpec((r, c), lambda i: (0, 0))
    return pl.pallas_call(
        _wo_kernel,
        grid=(t // tm,),
        in_specs=[row, row, full(d, d), full(1, d), full(d, LANES), full(d, LANES), full(1, LANES)],
        out_specs=[row, row, pl.BlockSpec((tm, LANES), lambda i: (i, 0))],
        out_shape=[jax.ShapeDtypeStruct((t, d), F32), jax.ShapeDtypeStruct((t, d), F32),
                   jax.ShapeDtypeStruct((t, LANES), F32)],
        compiler_params=_cp(("parallel",)),
        name="wo",
    )(x, merged, w_o.astype(BF16), g_ffn.reshape(1, d).astype(F32), wr_hi, wr_lo, br)


def _router_kernel(lg_ref, idx_ref, gate_ref, rank_ref, cnt_ref, carry_ref, *, tm):
    i = pl.program_id(0)

    @pl.when(i == 0)
    def _():
        carry_ref[...] = jnp.zeros_like(carry_ref)

    l = lg_ref[...]
    lane = lax.broadcasted_iota(jnp.int32, l.shape, 1).astype(F32)
    vals, sels, idxs = [], [], []
    onehot = jnp.zeros(l.shape, F32)
    for _ in range(TOP_K):
        m = jnp.max(l, axis=-1, keepdims=True)
        idx = jnp.min(jnp.where(l == m, lane, float(LANES)), axis=-1, keepdims=True)
        sel = lane == idx
        vals.append(m)
        sels.append(sel)
        idxs.append(idx)
        onehot = onehot + jnp.where(sel, 1.0, 0.0)
        l = jnp.where(sel, -jnp.inf, l)
    es = [jnp.exp(vv - vals[0]) for vv in vals]
    den = es[0] + es[1] + es[2] + es[3]

    trow = lax.broadcasted_iota(jnp.int32, (tm, tm), 0)
    tcol = lax.broadcasted_iota(jnp.int32, (tm, tm), 1)
    tri = jnp.where(tcol < trow, 1.0, 0.0).astype(BF16)
    cum = _mm(tri, onehot.astype(BF16)) + carry_ref[...]
    carry = carry_ref[...] + jnp.sum(onehot, axis=0, keepdims=True)
    carry_ref[...] = carry
    cnt_ref[...] = carry

    idx_out = jnp.zeros(l.shape, F32)
    gate_out = jnp.zeros(l.shape, F32)
    rank_out = jnp.zeros(l.shape, F32)
    for kk in range(TOP_K):
        rk = jnp.sum(jnp.where(sels[kk], cum, 0.0), axis=-1, keepdims=True)
        idx_out = jnp.where(lane == kk, idxs[kk], idx_out)
        gate_out = jnp.where(lane == kk, es[kk] / den, gate_out)
        rank_out = jnp.where(lane == kk, rk, rank_out)
    idx_ref[...] = idx_out.T[0:SUBLANES, :].astype(jnp.int32)
    gate_ref[...] = gate_out
    rank_ref[...] = rank_out.T[0:SUBLANES, :].astype(jnp.int32)


def router(logits, tm):
    t = logits.shape[0]
    spec = pl.BlockSpec((tm, LANES), lambda i: (i, 0))
    kspec = pl.BlockSpec((SUBLANES, tm), lambda i: (0, i))
    return pl.pallas_call(
        functools.partial(_router_kernel, tm=tm),
        grid=(t // tm,),
        in_specs=[spec],
        out_specs=[kspec, spec, kspec, pl.BlockSpec((1, LANES), lambda i: (0, 0))],
        out_shape=[jax.ShapeDtypeStruct((SUBLANES, t), jnp.int32), jax.ShapeDtypeStruct((t, LANES), F32),
                   jax.ShapeDtypeStruct((SUBLANES, t), jnp.int32), jax.ShapeDtypeStruct((1, LANES), F32)],
        scratch_shapes=[pltpu.VMEM((1, LANES), F32)],
        compiler_params=_cp(("arbitrary",)),
        name="router",
    )(logits)


def _dispatch_kernel(dest_ref, zrow_ref, meta_ref, x_ref, rows_ref, zero_ref, zsem, sem, *, tm, ne, n_sub_total,
                     n_tok):
    i = pl.program_id(0)

    @pl.when(i == 0)
    def _():
        zero_ref[...] = jnp.zeros_like(zero_ref)

        def zcopy(row0):
            return pltpu.make_async_copy(
                zero_ref, rows_ref.at[pl.ds(pl.multiple_of(row0, SUB_ROWS), SUB_ROWS), :], zsem)

        for e in range(ne):
            zcopy(zrow_ref[e]).start()

        def tail(sb, carry):
            zcopy(sb * SUB_ROWS).start()
            return carry

        lax.fori_loop(meta_ref[0], n_sub_total, tail, 0)
        for e in range(ne):
            zcopy(zrow_ref[e]).wait()

        def tail_wait(sb, carry):
            zcopy(sb * SUB_ROWS).wait()
            return carry

        lax.fori_loop(meta_ref[0], n_sub_total, tail_wait, 0)

    base = i * tm

    def row_copy(r, dst_row):
        return pltpu.make_async_copy(x_ref.at[pl.ds(r, 1), :], rows_ref.at[pl.ds(dst_row, 1), :], sem)

    for r in range(tm):
        for kk in range(TOP_K):
            row_copy(r, dest_ref[base + (kk * n_tok + r)]).start(priority=kk % 2)
    for kk in range(TOP_K):
        pltpu.make_async_copy(x_ref, rows_ref.at[pl.ds(0, tm), :], sem).wait()


def dispatch(xn, dest_flat, zrow, used_sub, n_rows, tm):
    t, d = xn.shape
    ne = zrow.shape[0]
    grid_spec = pltpu.PrefetchScalarGridSpec(
        num_scalar_prefetch=3,
        grid=(t // tm,),
        in_specs=[pl.BlockSpec((tm, d), lambda i, *_: (i, 0))],
        out_specs=pl.BlockSpec(memory_space=pl.ANY),
        scratch_shapes=[pltpu.VMEM((SUB_ROWS, d), xn.dtype), pltpu.SemaphoreType.DMA, pltpu.SemaphoreType.DMA],
    )
    return pl.pallas_call(
        functools.partial(_dispatch_kernel, tm=tm, ne=ne, n_sub_total=n_rows // SUB_ROWS, n_tok=t),
        grid_spec=grid_spec,
        out_shape=jax.ShapeDtypeStruct((n_rows, d), xn.dtype),
        compiler_params=_cp(("arbitrary",)),
        name="dispatch",
    )(dest_flat, zrow, used_sub, xn)


def _expert_kernel(ge_ref, r0_ref, ns_ref, meta_ref, rows_ref, wg_ref, bg_ref, wu_ref, bu_ref, wd_ref, bd_ref,
                   out_ref, stage_ref, xb_ref, acc_ref, ost_ref, xsem, osem, *, n_sub_total):
    g = pl.program_id(0)
    j = pl.program_id(1)
    nj = pl.num_programs(1)
    ng = meta_ref[1]
    used = g < ng
    nsub = ns_ref[g]
    d = stage_ref.shape[1]

    def x_copy(grp, sidx):
        row0 = pl.multiple_of(r0_ref[grp] + sidx * SUB_ROWS, SUB_ROWS)
        return pltpu.make_async_copy(rows_ref.at[pl.ds(row0, SUB_ROWS), :],
                                     stage_ref.at[pl.ds(sidx * SUB_ROWS, SUB_ROWS), :], xsem)

    def o_copy(row0, sidx):
        return pltpu.make_async_copy(ost_ref.at[pl.ds(sidx * SUB_ROWS, SUB_ROWS), :],
                                     out_ref.at[pl.ds(pl.multiple_of(row0, SUB_ROWS), SUB_ROWS), :], osem)

    def for_subs(count, fn):
        for sidx in range(GROUP_SUBS):
            @pl.when(sidx < count)
            def _():
                fn(sidx)

    @pl.when((g == 0) & (j == 0))
    def _():
        for_subs(ns_ref[0], lambda sidx: x_copy(0, sidx).start())
        ost_ref[0:SUB_ROWS, :] = jnp.zeros((SUB_ROWS, d), F32)

        def tail(sb, carry):
            cp = o_copy(sb * SUB_ROWS, 0)
            cp.start()
            cp.wait()
            return carry

        lax.fori_loop(meta_ref[0], n_sub_total, tail, 0)

    @pl.when(used & (j == 0))
    def _():
        for_subs(nsub, lambda sidx: x_copy(g, sidx).wait())

        def stage_in(sidx):
            sl = pl.ds(sidx * SUB_ROWS, SUB_ROWS)
            xb_ref[sl, :] = stage_ref[sl, :].astype(BF16)
            acc_ref[sl, :] = jnp.zeros((SUB_ROWS, d), F32)

        for_subs(nsub, stage_in)

        @pl.when(g + 1 < ng)
        def _():
            for_subs(ns_ref[g + 1], lambda sidx: x_copy(g + 1, sidx).start())

    @pl.when(used)
    def _():
        for n in range(1, GROUP_SUBS + 1):
            @pl.when(nsub == n)
            def _():
                rows = n * SUB_ROWS
                xb = xb_ref[0:rows, :]
                hg = jnp.minimum(_mm(xb, wg_ref[...].astype(BF16)) + bg_ref[...], SWIGLU_LIMIT)
                hu = jnp.clip(_mm(xb, wu_ref[...].astype(BF16)) + bu_ref[...], -SWIGLU_LIMIT, SWIGLU_LIMIT)
                hh = (hu + 1.0) * hg * _sigmoid(SWIGLU_ALPHA * hg)
                acc_ref[0:rows, :] += _mm(hh.astype(BF16), wd_ref[...].astype(BF16))

    @pl.when(used & (j == nj - 1))
    def _():
        @pl.when(g > 0)
        def _():
            for_subs(ns_ref[g - 1], lambda sidx: o_copy(r0_ref[g - 1] + sidx * SUB_ROWS, sidx).wait())

        def result_out(sidx):
            sl = pl.ds(sidx * SUB_ROWS, SUB_ROWS)
            ost_ref[sl, :] = acc_ref[sl, :] + bd_ref[...]
            o_copy(r0_ref[g] + sidx * SUB_ROWS, sidx).start()

        for_subs(nsub, result_out)

        @pl.when(g == ng - 1)
        def _():
            for_subs(nsub, lambda sidx: o_copy(r0_ref[g] + sidx * SUB_ROWS, sidx).wait())


def experts(rows, g_expert, g_row0, g_nsub, meta, w_gate, b_gate, w_up, b_up, w_down, b_down, tn):
    n_rows, d = rows.shape
    ne, _, de = w_gate.shape
    n_groups_max = g_expert.shape[0]
    nj = de // tn
    gr = GROUP_SUBS * SUB_ROWS

    def eb(g, ge, meta):
        return ge[jnp.minimum(g, meta[1] - 1)]

    def jb(g, j, meta):
        return jnp.where(g < meta[1], j, nj - 1)

    grid_spec = pltpu.PrefetchScalarGridSpec(
        num_scalar_prefetch=4,
        grid=(n_groups_max, nj),
        in_specs=[
            pl.BlockSpec(memory_space=pl.ANY),
            pl.BlockSpec((None, d, tn), lambda g, j, ge, r0, ns, mt: (eb(g, ge, mt), 0, jb(g, j, mt))),
            pl.BlockSpec((None, 1, tn), lambda g, j, ge, r0, ns, mt: (eb(g, ge, mt), 0, jb(g, j, mt))),
            pl.BlockSpec((None, d, tn), lambda g, j, ge, r0, ns, mt: (eb(g, ge, mt), 0, jb(g, j, mt))),
            pl.BlockSpec((None, 1, tn), lambda g, j, ge, r0, ns, mt: (eb(g, ge, mt), 0, jb(g, j, mt))),
            pl.BlockSpec((None, tn, d), lambda g, j, ge, r0, ns, mt: (eb(g, ge, mt), jb(g, j, mt), 0)),
            pl.BlockSpec((None, 1, d), lambda g, j, ge, r0, ns, mt: (eb(g, ge, mt), 0, 0)),
        ],
        out_specs=pl.BlockSpec(memory_space=pl.ANY),
        scratch_shapes=[
            pltpu.VMEM((gr, d), F32),
            pltpu.VMEM((gr, d), BF16),
            pltpu.VMEM((gr, d), F32),
            pltpu.VMEM((gr, d), F32),
            pltpu.SemaphoreType.DMA,
            pltpu.SemaphoreType.DMA,
        ],
    )
    return pl.pallas_call(
        functools.partial(_expert_kernel, n_sub_total=n_rows // SUB_ROWS),
        grid_spec=grid_spec,
        out_shape=jax.ShapeDtypeStruct((n_rows, d), F32),
        compiler_params=_cp(("arbitrary", "arbitrary")),
        name="experts",
    )(g_expert, g_row0, g_nsub, meta, rows, w_gate, b_gate.reshape(ne, 1, de), w_up, b_up.reshape(ne, 1, de),
      w_down, b_down.reshape(ne, 1, d))


def _combine_kernel(dest_ref, rows_ref, h_ref, gate_ref, g_ref, o_ref, buf_ref, sem, *, tm, n_tok):
    i = pl.program_id(0)
    nt = pl.num_programs(0)
    n = tm * TOP_K
    slot = i % 2

    def issue_tile(tile, slt):
        base = tile * tm
        for r in range(tm):
            for kk in range(TOP_K):
                pltpu.make_async_copy(rows_ref.at[pl.ds(dest_ref[base + (kk * n_tok + r)], 1), :],
                                      buf_ref.at[slt, pl.ds(kk * tm + r, 1), :], sem.at[slt]
                                      ).start(priority=kk % 2)

    @pl.when(i == 0)
    def _():
        issue_tile(0, 0)

    for par in range(2):
        @pl.when(slot == par)
        def _():
            pltpu.make_async_copy(rows_ref.at[pl.ds(0, n), :], buf_ref.at[par], sem.at[par]).wait()

            @pl.when(i + 1 < nt)
            def _():
                issue_tile(i + 1, 1 - par)

    gates = gate_ref[...]
    h = h_ref[...]
    for kk in range(TOP_K):
        h = h + gates[:, kk:kk + 1] * buf_ref[slot, kk * tm:(kk + 1) * tm, :]
    ms = jnp.mean(h * h, axis=-1, keepdims=True)
    o_ref[...] = (h * lax.rsqrt(ms + EPS) * g_ref[...]).astype(o_ref.dtype)


def combine(dest_flat, out_rows, h1, gates, g_final, tm, out_dtype):
    t, d = h1.shape
    grid_spec = pltpu.PrefetchScalarGridSpec(
        num_scalar_prefetch=1,
        grid=(t // tm,),
        in_specs=[
            pl.BlockSpec(memory_space=pl.ANY),
            pl.BlockSpec((tm, d), lambda i, *_: (i, 0)),
            pl.BlockSpec((tm, LANES), lambda i, *_: (i, 0)),
            pl.BlockSpec((1, d), lambda i, *_: (0, 0)),
        ],
        out_specs=pl.BlockSpec((tm, d), lambda i, *_: (i, 0)),
        scratch_shapes=[pltpu.VMEM((2, tm * TOP_K, d), out_rows.dtype), pltpu.SemaphoreType.DMA((2,))],
    )
    return pl.pallas_call(
        functools.partial(_combine_kernel, tm=tm, n_tok=t),
        grid_spec=grid_spec,
        out_shape=jax.ShapeDtypeStruct((t, d), out_dtype),
        compiler_params=_cp(("arbitrary",)),
        name="combine",
    )(dest_flat, out_rows, h1, gates, g_final.reshape(1, d).astype(F32))


def _tile(n, pref):
    while n % pref:
        pref //= 2
    return pref


def _group_tables(counts, n_groups_max):
    ne = counts.shape[0]
    nsub_e = (counts + SUB_ROWS - 1) // SUB_ROWS
    pend_sub = jnp.cumsum(nsub_e)
    pstart = (pend_sub - nsub_e) * SUB_ROWS
    ng_e = (nsub_e + GROUP_SUBS - 1) // GROUP_SUBS
    base_e = nsub_e // jnp.maximum(ng_e, 1)
    rem_e = nsub_e - base_e * ng_e
    cum_ng = jnp.cumsum(ng_e)
    n_groups = cum_ng[-1]
    g = jnp.arange(n_groups_max, dtype=jnp.int32)
    e_g = jnp.minimum(jnp.sum((g[:, None] >= cum_ng[None, :]).astype(jnp.int32), axis=1), ne - 1)
    lg = g - (cum_ng[e_g] - ng_e[e_g])
    valid = g < n_groups
    g_nsub = jnp.where(valid, base_e[e_g] + (lg < rem_e[e_g]).astype(jnp.int32), 0)
    g_row0 = jnp.where(valid, pstart[e_g] + SUB_ROWS * (lg * base_e[e_g] + jnp.minimum(lg, rem_e[e_g])), 0)
    zrow = pstart + jnp.maximum(nsub_e - 1, 0) * SUB_ROWS
    meta = jnp.stack([pend_sub[-1], n_groups]).astype(jnp.int32)
    i32 = lambda z: z.astype(jnp.int32)
    return i32(pstart), i32(zrow), i32(e_g), i32(g_row0), i32(g_nsub), meta


def kernel(x, mem, g_mix, w_in, mu_rwkv, rwkv_w0, rwkv_w2, rwkv_a0, rwkv_a2, rwkv_g2, rwkv_k_k, rwkv_k_a,
           rwkv_r_k, rwkv_ln_w, rwkv_ln_b, w_rwkv_out, lru_conv_w, lru_conv_b, lru_w_r, lru_b_r, lru_w_i,
           lru_b_i, lru_lambda, w_lru_out, g_mem, w_mem_k, w_mem_v, w_xa_out, w_o, g_ffn, w_router,
           b_router, w_gate, b_gate, w_up, b_up, w_down, b_down, g_final):
    bsz, seq, d = x.shape
    assert w_in.shape[0] == 1, "single-layer block"
    l = 0
    t = bsz * seq
    c = rwkv_w0.shape[1]
    n_lora = LORA_W + LORA_A + LORA_G
    ne = w_router.shape[2]
    assert lru_conv_b.shape[1] == c and w_mem_k.shape[2] == c and c % LORA_PAD == 0

    h = x.reshape(t, d).astype(F32)
    memf = mem.reshape(bsz * mem.shape[1], d)
    wi = w_in[l].astype(BF16)
    c0 = 3 * c + n_lora
    w_a = jnp.concatenate(
        [wi[:, 0:3 * c], wi[:, c0:c0 + 2 * c], wi[:, 3 * c:c0], jnp.zeros((d, LORA_PAD - n_lora), BF16)], axis=1)
    w_b = wi[:, c0 + 2 * c:]

    p_a = norm_matmul(h, g_mix[l], w_a, _tile(t, 1024), _tile(w_a.shape[1], 512), F32)
    p_b = norm_matmul(h, g_mix[l], w_b, _tile(t, 1024), _tile(w_b.shape[1], 1024), BF16)
    mn = norm_cast(memf, g_mem[l], _tile(memf.shape[0], 256))
    kv = matmul(mn, jnp.concatenate([w_mem_k[l], w_mem_v[l]], axis=1).astype(BF16),
                _tile(memf.shape[0], 256), _tile(2 * c, 512), BF16)

    y_a = rwkv_branch(p_a, bsz, seq, mu_rwkv[l], rwkv_w0[l], rwkv_w2[l], rwkv_a0[l], rwkv_a2[l],
                      rwkv_g2[l], rwkv_k_k[l], rwkv_k_a[l], rwkv_r_k[l].reshape(-1), rwkv_ln_w[l],
                      rwkv_ln_b[l], _tile(seq, 256), 5 * c)
    y_b = lru_branch(p_a, bsz, seq, 3 * c, 4 * c, lru_conv_w[l], lru_conv_b[l], lru_w_r[l], lru_b_r[l],
                     lru_w_i[l], lru_b_i[l], lru_lambda[l], _tile(seq, 256))
    y_c = xattn_branch(p_b, kv, bsz, seq, _tile(seq, 512))
    merged = merge(y_a, y_b, y_c, w_rwkv_out[l].astype(BF16), w_lru_out[l].astype(BF16),
                   w_xa_out[l].astype(BF16), p_b, c, _tile(t, 512), _tile(d, 1024))
    h1, xn2, logits = wo_block(h, merged, w_o[l], g_ffn[l], w_router[l], b_router[l], _tile(t, 256))

    idx, gates, rank, counts = router(logits, _tile(t, 512))
    counts = counts[0, :ne].astype(jnp.int32)
    n_rows = t * TOP_K + ne * SUB_ROWS
    n_groups_max = (t * TOP_K) // (GROUP_SUBS * SUB_ROWS) + ne
    pstart, zrow, g_expert, g_row0, g_nsub, meta = _group_tables(counts, n_groups_max)
    e_kt = idx[:TOP_K]
    first_row = jnp.zeros_like(e_kt)
    for e in range(ne):
        first_row = jnp.where(e_kt == e, pstart[e], first_row)
    dest = (first_row + rank[:TOP_K]).reshape(-1)

    tmd = _tile(t, 256)
    rows = dispatch(xn2, dest, zrow, meta, n_rows, tmd)
    out_rows = experts(rows, g_expert, g_row0, g_nsub, meta, w_gate[l], b_gate[l], w_up[l], b_up[l],
                       w_down[l], b_down[l], _tile(w_gate.shape[3], 256))
    out = combine(dest, out_rows, h1, gates, g_final, tmd, x.dtype)
    return out.reshape(bsz, seq, d)
```
